```python
import math
import jax, jax.numpy as jnp
from jax import lax
import numpy as np

D_MODEL = 4096
BATCH = 32
SEQ = 256
DEPTH = 2
DEC_BATCH = 8
DEC_SEQ = 2048
PAST_LEN = 256

GRID_W = 64
N_MIXERS = 2
N_A = (DEPTH + 1) // 2
N_B = DEPTH // 2
EPS = 1e-6
M_HEADS = 8
M_DV = D_MODEL // M_HEADS
M_DQK = M_DV // 2
M_CHUNK = 64
M_QK = M_HEADS * M_DQK
M_V = M_HEADS * M_DV
M_IN = 2 * M_QK + 2 * M_V + 4 * M_HEADS
A_HEADS = 16
A_DH = D_MODEL // (2 * A_HEADS)
A_QB = 128
ROPE_THETA = 10000.0
N_EXPERTS = 64
N_GROUPS = 8
EXPERTS_PER_GROUP = N_EXPERTS // N_GROUPS
TOP_K = 2
D_FF = D_MODEL // 4
MOE_BLOCK = 128

kernel_name = 'hybrid_mlstm_diffattn_moe_diffusion_step'


def rmsnorm(x, g):
    xf = x.astype(jnp.float32)
    y = xf * lax.rsqrt(jnp.mean(xf * xf, axis=-1, keepdims=True) + EPS)
    return (y * g.astype(jnp.float32)).astype(x.dtype)


def adaln(cvec, w, b):
    m = jax.nn.silu(cvec) @ w + b
    return [a[:, None, :] for a in jnp.split(m, 6, axis=-1)]


def modulate(x, g, shift, scale):
    return rmsnorm(x, g) * (1 + scale) + shift


def axial_rope(x):
    T = x.shape[1]
    rows = T // GRID_W
    row = jnp.repeat(jnp.arange(rows, dtype=jnp.float32), GRID_W)
    col = jnp.tile(jnp.arange(GRID_W, dtype=jnp.float32), rows)
    half = A_DH // 2
    inv = ROPE_THETA ** (-jnp.arange(0, half, 2, dtype=jnp.float32) / half)

    def rot(xh, pos):
        ang = pos[:, None] * inv
        cos = jnp.cos(ang)[:, None, None, :]
        sin = jnp.sin(ang)[:, None, None, :]
        x1, x2 = jnp.split(xh, 2, axis=-1)
        return jnp.concatenate([x1 * cos - x2 * sin, x1 * sin + x2 * cos], axis=-1)

    xf = x.astype(jnp.float32)
    return jnp.concatenate([rot(xf[..., :half], row), rot(xf[..., half:], col)], axis=-1).astype(x.dtype)


def mlstm_project(h, w_in, b_gate):
    B_, T_ = h.shape[:2]
    z = h @ w_in
    q, k, v, o, g = jnp.split(z, [M_QK, 2 * M_QK, 2 * M_QK + M_V, 2 * M_QK + 2 * M_V], axis=-1)
    heads = lambda a, d: jnp.transpose(a.reshape(B_, T_, M_HEADS, d), (0, 2, 1, 3)).astype(jnp.float32)
    q = heads(q, M_DQK)
    k = heads(k, M_DQK) * (M_DQK ** -0.5)
    v = heads(v, M_DV)
    g = (g + b_gate).astype(jnp.float32).reshape(B_, T_, 2, 2, M_HEADS)
    ig = jnp.transpose(g[:, :, :, 0], (0, 2, 3, 1))
    lf = jnp.transpose(jax.nn.log_sigmoid(g[:, :, :, 1]), (0, 2, 3, 1))
    return q, k, v, o, ig, lf


def mlstm_scan(q, k, v, ig, lf, C0, n0, m0):
    B_, H_, T_, _ = q.shape
    dv = v.shape[-1]
    nc = T_ // M_CHUNK

    def chunks(a):
        return jnp.moveaxis(a.reshape(B_, H_, nc, M_CHUNK, *a.shape[3:]), 2, 0)

    tril = jnp.tril(jnp.ones((M_CHUNK, M_CHUNK), dtype=bool))

    def step(carry, inp):
        C, n, m = carry
        qc, kc, vc, igc, lfc = inp
        b = jnp.cumsum(lfc, axis=-1)
        log_d = jnp.where(tril, b[..., :, None] - b[..., None, :] + igc[..., None, :], -jnp.inf)
        log_inter = b + m[..., None]
        m_t = jnp.maximum(log_inter, jnp.max(log_d, axis=-1))
        s = jnp.einsum('bhtd,bhsd->bhts', qc, kc) * jnp.exp(log_d - m_t[..., None])
        w_inter = jnp.exp(log_inter - m_t)
        num = w_inter[..., None] * jnp.einsum('bhtd,bhde->bhte', qc, C) + jnp.einsum('bhts,bhse->bhte', s, vc)
        den = w_inter * jnp.einsum('bhtd,bhd->bht', qc, n) + jnp.sum(s, axis=-1)
        h = num / jnp.maximum(jnp.abs(den), jnp.exp(-m_t))[..., None]
        b_last = b[..., -1]
        log_w = b_last[..., None] - b + igc
        m_new = jnp.maximum(b_last + m, jnp.max(log_w, axis=-1))
        w_s = jnp.exp(log_w - m_new[..., None])
        decay = jnp.exp(b_last + m - m_new)
        C_new = decay[..., None, None] * C + jnp.einsum('bhsd,bhse->bhde', kc * w_s[..., None], vc)
        n_new = decay[..., None] * n + jnp.einsum('bhs,bhsd->bhd', w_s, kc)
        return (C_new, n_new, m_new), h

    (C, n, m), h = lax.scan(step, (C0, n0, m0), (chunks(q), chunks(k), chunks(v), chunks(ig), chunks(lf)))
    h = jnp.moveaxis(h, 0, 2).reshape(B_, H_, T_, dv)
    return h, C, n, m


def mlstm_bidir(q, k, v, ig, lf, init_f, init_b):
    h_f, C_f, n_f, m_f = mlstm_scan(q, k, v, ig[:, 0], lf[:, 0], *init_f)
    flip = lambda a: jnp.flip(a, axis=2)
    h_b, C_b, n_b, m_b = mlstm_scan(flip(q), flip(k), flip(v), jnp.flip(ig[:, 1], axis=-1),
                                    jnp.flip(lf[:, 1], axis=-1), *init_b)
    return h_f + flip(h_b), (C_f, n_f, m_f), (C_b, n_b, m_b)


def mlstm_out(hsum, o, g_head, w_out):
    B_, H_, T_, dv = hsum.shape
    hn = rmsnorm(jnp.transpose(hsum, (0, 2, 1, 3)), g_head).reshape(B_, T_, H_ * dv)
    return (hn.astype(o.dtype) * jax.nn.sigmoid(o)) @ w_out


def diff_project(h, w_qkv):
    B_, T_ = h.shape[:2]
    q, k, v = jnp.split(h @ w_qkv, 3, axis=-1)
    return (q.reshape(B_, T_, A_HEADS, 2, A_DH), k.reshape(B_, T_, A_HEADS, 2, A_DH),
            v.reshape(B_, T_, A_HEADS, 2 * A_DH))


def diff_lambda(lp, lam_init):
    lp = lp.astype(jnp.float32)
    return jnp.exp(jnp.sum(lp[0] * lp[1])) - jnp.exp(jnp.sum(lp[2] * lp[3])) + lam_init


def diff_attention(q, k, v, lam):
    B_, T_, H_, _, dh = q.shape
    nb = T_ // A_QB
    qb = jnp.moveaxis(q.reshape(B_, nb, A_QB, H_, 2, dh), 1, 0)

    def block(qi):
        s = jnp.einsum('bqhcd,bkhcd->bhcqk', qi, k).astype(jnp.float32) * (dh ** -0.5)
        p = jax.nn.softmax(s, axis=-1).astype(v.dtype)
        o = jnp.einsum('bhcqk,bkhe->bqhce', p, v)
        return o[:, :, :, 0] - lam.astype(o.dtype) * o[:, :, :, 1]

    o = lax.map(block, qb)
    return jnp.moveaxis(o, 0, 1).reshape(B_, T_, H_, v.shape[-1])


def diff_out(o, g_sub, w_o, lam_init):
    B_, T_ = o.shape[:2]
    return (rmsnorm(o, g_sub) * (1.0 - lam_init)).reshape(B_, T_, -1) @ w_o


def route(xt, router_w, router_b):
    probs = jax.nn.softmax((xt @ router_w).astype(jnp.float32), axis=-1)
    sel = probs + router_b.astype(jnp.float32)
    grp = sel.reshape(-1, N_GROUPS, EXPERTS_PER_GROUP)
    gscore = jnp.sum(lax.top_k(grp, TOP_K)[0], axis=-1)
    gi = jnp.argmax(gscore, axis=-1)
    in_group = jnp.take_along_axis(grp, gi[:, None, None], axis=1)[:, 0]
    _, local = lax.top_k(in_group, TOP_K)
    idx = gi[:, None] * EXPERTS_PER_GROUP + local
    w = jnp.take_along_axis(probs, idx, axis=1)
    return idx, w / jnp.sum(w, axis=-1, keepdims=True)


def moe_ffn(h, router_w, router_b, w_gate, w_up, w_down):
    B_, T_, D_ = h.shape
    xt = h.reshape(B_ * T_, D_)
    N = xt.shape[0]
    A = N * TOP_K
    idx, gw = route(xt, router_w, router_b)
    flat_e = idx.reshape(-1)
    flat_tok = jnp.repeat(jnp.arange(N, dtype=jnp.int32), TOP_K)
    order = jnp.argsort(flat_e, stable=True)
    sorted_e = flat_e[order]
    counts = jnp.bincount(flat_e, length=N_EXPERTS)
    padded = ((counts + MOE_BLOCK - 1) // MOE_BLOCK) * MOE_BLOCK
    pad_end = jnp.cumsum(padded)
    pad_start = pad_end - padded
    start = jnp.cumsum(counts) - counts
    dest_sorted = pad_start[sorted_e] + (jnp.arange(A) - start[sorted_e])
    n_slots = (-(-A // MOE_BLOCK) + N_EXPERTS) * MOE_BLOCK
    n_blocks = n_slots // MOE_BLOCK
    slot_tok = jnp.full((n_slots,), N, dtype=jnp.int32).at[dest_sorted].set(flat_tok[order])
    block_e = jnp.minimum(jnp.searchsorted(pad_end, jnp.arange(n_blocks) * MOE_BLOCK, side='right'),
                          N_EXPERTS - 1)
    x_pad = jnp.concatenate([xt, jnp.zeros((1, D_), xt.dtype)], axis=0)
    xb = x_pad[slot_tok].reshape(n_blocks, MOE_BLOCK, D_)

    def expert_block(args):
        xblk, e = args
        return (jax.nn.silu(xblk @ w_gate[e]) * (xblk @ w_up[e])) @ w_down[e]

    yb = lax.map(expert_block, (xb, block_e)).reshape(n_slots, D_)
    dest = jnp.zeros((A,), dtype=dest_sorted.dtype).at[order].set(dest_sorted)
    y = jnp.sum(yb[dest].reshape(N, TOP_K, D_) * gw[..., None].astype(yb.dtype), axis=1)
    return y.reshape(B_, T_, D_)


def setup_inputs(seed: int = 0) -> dict:
    key = jax.random.key(seed)
    ks = jax.random.split(key, 32)
    D = D_MODEL
    nrm = lambda k, shape, s: jax.random.normal(k, shape, jnp.float32) * s
    i_b = nrm(ks[20], (N_A, 2, 1, M_HEADS), 0.1)
    f_b = 3.0 + 3.0 * jax.random.uniform(ks[21], (N_A, 2, 1, M_HEADS), jnp.float32)
    return {
        'x_prompt': nrm(ks[0], (BATCH, SEQ, D), 1.0),
        'x_sample': nrm(ks[1], (DEC_BATCH, DEC_SEQ, D), 1.0),
        'state_C': nrm(ks[2], (DEC_BATCH, N_A, 2, M_HEADS, M_DQK, M_DV), 0.05),
        'state_n': nrm(ks[3], (DEC_BATCH, N_A, 2, M_HEADS, M_DQK), 0.5),
        'state_m': nrm(ks[4], (DEC_BATCH, N_A, 2, M_HEADS), 1.0),
        'cache_k': nrm(ks[5], (DEC_BATCH, N_B, PAST_LEN, A_HEADS, 2, A_DH), 1.0),
        'cache_v': nrm(ks[6], (DEC_BATCH, N_B, PAST_LEN, A_HEADS, 2 * A_DH), 1.0),
        'c': nrm(ks[7], (DEC_BATCH, D), 1.0),
        'c_ctx': nrm(ks[8], (D,), 1.0),
        'w_mod': nrm(ks[9], (DEPTH, D, 6 * D), 0.5 * D ** -0.5),
        'b_mod': nrm(ks[10], (DEPTH, 6 * D), 0.02),
        'g_norm': 1.0 + nrm(ks[11], (DEPTH, 2, D), 0.02),
        'm_w_in': nrm(ks[12], (N_A, D, M_IN), D ** -0.5),
        'm_b_gate': jnp.concatenate([i_b, f_b], axis=2).reshape(N_A, 4 * M_HEADS),
        'm_g_head': 1.0 + nrm(ks[13], (N_A, M_HEADS, M_DV), 0.02),
        'm_w_out': nrm(ks[14], (N_A, M_V, D), M_V ** -0.5),
        'a_w_qkv': nrm(ks[15], (N_B, D, 3 * D), D ** -0.5),
        'a_lam': nrm(ks[16], (N_B, 4, A_DH), 0.1),
        'a_g_sub': 1.0 + nrm(ks[17], (N_B, A_HEADS, 2 * A_DH), 0.02),
        'a_w_o': nrm(ks[18], (N_B, D, D), D ** -0.5),
        'router_w': nrm(ks[19], (D, N_EXPERTS), D ** -0.5),
        'router_b': nrm(ks[22], (N_EXPERTS,), 0.01),
        'w_gate': nrm(ks[23], (DEPTH, N_EXPERTS, D, D_FF), D ** -0.5),
        'w_up': nrm(ks[24], (DEPTH, N_EXPERTS, D, D_FF), D ** -0.5),
        'w_down': nrm(ks[25], (DEPTH, N_EXPERTS, D_FF, D), D_FF ** -0.5),
        'g_final': 1.0 + nrm(ks[26], (D,), 0.02),
    }


def reference(x_prompt, x_sample, state_C, state_n, state_m, cache_k, cache_v, c, c_ctx,
              w_mod, b_mod, g_norm, m_w_in, m_b_gate, m_g_head, m_w_out,
              a_w_qkv, a_lam, a_g_sub, a_w_o, router_w, router_b, w_gate, w_up, w_down, g_final):
    f32 = lambda a: a.astype(jnp.float32)
    xp, xs = x_prompt, x_sample
    Bp = xp.shape[0]
    new_C, new_n, new_m, new_k, new_v = [], [], [], [], []
    for i in range(DEPTH):
        j = i // N_MIXERS
        sh1p, sc1p, g1p, sh2p, sc2p, g2p = adaln(c_ctx[None, :], w_mod[i], b_mod[i])
        sh1s, sc1s, g1s, sh2s, sc2s, g2s = adaln(c, w_mod[i], b_mod[i])
        hp = modulate(xp, g_norm[i, 0], sh1p, sc1p)
        hs = modulate(xs, g_norm[i, 0], sh1s, sc1s)
        if i % N_MIXERS == 0:
            qp, kp, vp, op, igp, lfp = mlstm_project(hp, m_w_in[j], m_b_gate[j])
            z0 = (jnp.zeros((Bp, M_HEADS, M_DQK, M_DV), jnp.float32),
                  jnp.zeros((Bp, M_HEADS, M_DQK), jnp.float32),
                  jnp.zeros((Bp, M_HEADS), jnp.float32))
            hsum_p, sf, sb = mlstm_bidir(qp, kp, vp, igp, lfp, z0, z0)
            new_C.append(jnp.stack([sf[0], sb[0]], axis=1).astype(xp.dtype))
            new_n.append(jnp.stack([sf[1], sb[1]], axis=1).astype(xp.dtype))
            new_m.append(jnp.stack([sf[2], sb[2]], axis=1).astype(xp.dtype))
            init_f = (f32(state_C[:, j, 0]), f32(state_n[:, j, 0]), f32(state_m[:, j, 0]))
            init_b = (f32(state_C[:, j, 1]), f32(state_n[:, j, 1]), f32(state_m[:, j, 1]))
            qs, ks_, vs, os_, igs, lfs = mlstm_project(hs, m_w_in[j], m_b_gate[j])
            hsum_s, _, _ = mlstm_bidir(qs, ks_, vs, igs, lfs, init_f, init_b)
            mix_p = mlstm_out(hsum_p.astype(xp.dtype), op, m_g_head[j], m_w_out[j])
            mix_s = mlstm_out(hsum_s.astype(xs.dtype), os_, m_g_head[j], m_w_out[j])
        else:
            lam_init = 0.8 - 0.6 * math.exp(-0.3 * i)
            lam = diff_lambda(a_lam[j], lam_init)
            qp, kp, vp = diff_project(hp, a_w_qkv[j])
            new_k.append(kp)
            new_v.append(vp)
            op = diff_attention(qp, kp, vp, lam)
            qs, ks_, vs = diff_project(hs, a_w_qkv[j])
            qs = axial_rope(qs)
            ks_ = axial_rope(ks_)
            k_all = jnp.concatenate([ks_, cache_k[:, j].astype(ks_.dtype)], axis=1)
            v_all = jnp.concatenate([vs, cache_v[:, j].astype(vs.dtype)], axis=1)
            os_ = diff_attention(qs, k_all, v_all, lam)
            mix_p = diff_out(op, a_g_sub[j], a_w_o[j], lam_init)
            mix_s = diff_out(os_, a_g_sub[j], a_w_o[j], lam_init)
        xp = xp + g1p * mix_p
        xs = xs + g1s * mix_s
        hp = modulate(xp, g_norm[i, 1], sh2p, sc2p)
        hs = modulate(xs, g_norm[i, 1], sh2s, sc2s)
        xp = xp + g2p * moe_ffn(hp, router_w, router_b, w_gate[i], w_up[i], w_down[i])
        xs = xs + g2s * moe_ffn(hs, router_w, router_b, w_gate[i], w_up[i], w_down[i])
    y_prompt = rmsnorm(xp, g_final)
    y_sample = rmsnorm(xs, g_final)
    new_state_C = jnp.stack(new_C, axis=1)
    new_state_n = jnp.stack(new_n, axis=1)
    new_state_m = jnp.stack(new_m, axis=1)
    new_cache_k = jnp.stack(new_k, axis=1)
    new_cache_v = jnp.stack(new_v, axis=1)
    return (y_prompt, y_sample, new_state_C, new_state_n, new_state_m, new_cache_k, new_cache_v)
```

```python
import functools
import math

import jax
import jax.numpy as jnp
from jax import lax
from jax.experimental import pallas as pl
from jax.experimental.pallas import tpu as pltpu

F32 = jnp.float32
BF16 = jnp.bfloat16
I32 = jnp.int32

EPS = 1e-6
N_GROUPS = 8
GRID_W = 64
ROPE_THETA = 10000.0
N_MIXERS = 2

V7X_LANES = 128
V7X_SUBLANES = 8
V7X_VMEM_LIMIT = 56 * 1024 * 1024

MLSTM_CHUNK = 256
MOE_BM = 256
MOE_TF = 256
MOE_TN = 1024
ATTN_TQ = 256

NT_DIMS = (((1,), (1,)), ((), ()))
TN_DIMS = (((0,), (0,)), ((), ()))


def _cparams(sem):
    return pltpu.CompilerParams(dimension_semantics=sem, vmem_limit_bytes=V7X_VMEM_LIMIT)


def _silu(x):
    return x * jax.nn.sigmoid(x)


def _rms(x):
    return x * lax.rsqrt(jnp.mean(x * x, axis=-1, keepdims=True) + EPS)


def _adaln_kernel(cv_ref, w_ref, b_ref, o_ref):
    s = _silu(cv_ref[...]).astype(BF16)
    o_ref[...] = jnp.dot(s, w_ref[...].astype(BF16), preferred_element_type=F32) + b_ref[...]


def adaln_all(cv, w_mod, b_mod, tn=512):
    depth, d, n6 = w_mod.shape
    r = cv.shape[0]
    return pl.pallas_call(
        _adaln_kernel,
        grid=(depth, n6 // tn),
        in_specs=[
            pl.BlockSpec((r, d), lambda l, j: (0, 0)),
            pl.BlockSpec((None, d, tn), lambda l, j: (l, 0, j)),
            pl.BlockSpec((None, 1, tn), lambda l, j: (l, 0, j)),
        ],
        out_specs=pl.BlockSpec((None, r, tn), lambda l, j: (l, 0, j)),
        out_shape=jax.ShapeDtypeStruct((depth, r, n6), F32),
        compiler_params=_cparams(("arbitrary", "arbitrary")),
    )(cv, w_mod, b_mod.reshape(depth, 1, n6))


class Geo:
    def __init__(self, bp, tp, bs, ts):
        self.bp, self.tp, self.bs, self.ts = bp, tp, bs, ts
        self.np_ = bp * tp
        self.ns = bs * ts
        self.n = self.np_ + self.ns
        self.tm = min(1024, math.gcd(self.np_, ts))

    def mod_row(self, i, tm):
        r0 = i * tm
        return jnp.where(r0 < self.np_, 0, 1 + (r0 - self.np_) // self.ts)


def _modnorm_kernel(x_ref, g_ref, sh_ref, sc_ref, o_ref):
    y = _rms(x_ref[...]) * g_ref[...]
    o_ref[...] = (y * (1.0 + sc_ref[...]) + sh_ref[...]).astype(o_ref.dtype)


def modnorm(x, g_norm4, mod, geo, layer, which_g, which_sh, which_sc, tm=256):
    n, d = x.shape
    return pl.pallas_call(
        _modnorm_kernel,
        grid=(n // tm,),
        in_specs=[
            pl.BlockSpec((tm, d), lambda i: (i, 0)),
            pl.BlockSpec((None, None, 1, d), lambda i: (layer, which_g, 0, 0)),
            pl.BlockSpec((None, None, None, 1, d), lambda i: (layer, geo.mod_row(i, tm), which_sh, 0, 0)),
            pl.BlockSpec((None, None, None, 1, d), lambda i: (layer, geo.mod_row(i, tm), which_sc, 0, 0)),
        ],
        out_specs=pl.BlockSpec((tm, d), lambda i: (i, 0)),
        out_shape=jax.ShapeDtypeStruct((n, d), BF16),
        compiler_params=_cparams(("arbitrary",)),
    )(x, g_norm4, mod, mod)


def _mm_kernel(a_ref, w_ref, o_ref):
    o_ref[...] = jnp.dot(a_ref[...], w_ref[...].astype(BF16),
                         preferred_element_type=F32).astype(o_ref.dtype)


def _mm_res_kernel(a_ref, w_ref, x_ref, g_ref, o_ref):
    acc = jnp.dot(a_ref[...], w_ref[...].astype(BF16), preferred_element_type=F32)
    o_ref[...] = x_ref[...] + g_ref[...] * acc


def matmul(a, w3, widx, n_cols, tm, tn=512, out_dtype=F32):
    m, k = a.shape
    return pl.pallas_call(
        _mm_kernel,
        grid=(m // tm, n_cols // tn),
        in_specs=[
            pl.BlockSpec((tm, k), lambda i, j: (i, 0)),
            pl.BlockSpec((None, k, tn), lambda i, j: (widx, 0, j)),
        ],
        out_specs=pl.BlockSpec((tm, tn), lambda i, j: (i, j)),
        out_shape=jax.ShapeDtypeStruct((m, n_cols), out_dtype),
        compiler_params=_cparams(("arbitrary", "arbitrary")),
    )(a, w3)


def matmul_residual(a, w3, widx, x, mod, geo, layer, which_gate, tm, tn=512):
    m, k = a.shape
    d = w3.shape[2]
    return pl.pallas_call(
        _mm_res_kernel,
        grid=(m // tm, d // tn),
        in_specs=[
            pl.BlockSpec((tm, k), lambda i, j: (i, 0)),
            pl.BlockSpec((None, k, tn), lambda i, j: (widx, 0, j)),
            pl.BlockSpec((tm, tn), lambda i, j: (i, j)),
            pl.BlockSpec((None, None, None, 1, tn),
                         lambda i, j: (layer, geo.mod_row(i, tm), which_gate, 0, j)),
        ],
        out_specs=pl.BlockSpec((tm, tn), lambda i, j: (i, j)),
        out_shape=jax.ShapeDtypeStruct((m, d), F32),
        compiler_params=_cparams(("arbitrary", "arbitrary")),
    )(a, w3, x, mod)


def _gates_kernel(h_ref, wi_ref, wf_ref, bi_ref, bf_ref, col_ref, row_ref, *, chunk):
    h = h_ref[...]
    tm = h.shape[0]
    gi = jnp.dot(h, wi_ref[...], preferred_element_type=F32) + bi_ref[...]
    xf = jnp.dot(h, wf_ref[...], preferred_element_type=F32) + bf_ref[...]
    lf = jnp.minimum(xf, 0.0) - jnp.log1p(jnp.exp(-jnp.abs(xf)))
    r = lax.broadcasted_iota(I32, (tm, V7X_LANES), 0) % chunk
    lane = lax.broadcasted_iota(I32, (tm, V7X_LANES), 1)
    pre = lf
    suf = lf
    s = 1
    while s < chunk:
        pre = pre + jnp.where(r >= s, pltpu.roll(pre, s, 0), 0.0)
        suf = suf + jnp.where(r < chunk - s, pltpu.roll(suf, tm - s, 0), 0.0)
        s *= 2
    b = jnp.where(lane < V7X_LANES // 2, pre, suf)
    sub = lane % V7X_SUBLANES
    col = jnp.where(sub == 0, gi - b, jnp.where(sub == 1, b, gi))
    col_ref[...] = col
    row_ref[...] = col.T


def mlstm_gates(h, wi, wf, bi, bf, chunk, tm):
    n, d = h.shape
    return pl.pallas_call(
        functools.partial(_gates_kernel, chunk=chunk),
        grid=(n // tm,),
        in_specs=[
            pl.BlockSpec((tm, d), lambda i: (i, 0)),
            pl.BlockSpec((d, V7X_LANES), lambda i: (0, 0)),
            pl.BlockSpec((d, V7X_LANES), lambda i: (0, 0)),
            pl.BlockSpec((1, V7X_LANES), lambda i: (0, 0)),
            pl.BlockSpec((1, V7X_LANES), lambda i: (0, 0)),
        ],
        out_specs=[
            pl.BlockSpec((tm, V7X_LANES), lambda i: (i, 0)),
            pl.BlockSpec((V7X_LANES, tm), lambda i: (0, i)),
        ],
        out_shape=[jax.ShapeDtypeStruct((n, V7X_LANES), F32),
                   jax.ShapeDtypeStruct((V7X_LANES, n), F32)],
        compiler_params=_cparams(("arbitrary",)),
    )(h, wi, wf, bi, bf)


def _lane_pick(gc, lane_idx):
    lane = lax.broadcasted_iota(I32, gc.shape, 1)
    return jnp.sum(jnp.where(lane == lane_idx, gc, 0.0), axis=1, keepdims=True)


def _mlstm_chunk(qf, kf, vb, b_col, ig_col, a_row, state, reverse):
    L = qf.shape[0]
    qb = qf.astype(BF16)
    kb = kf.astype(BF16)
    ti = lax.broadcasted_iota(I32, (L, L), 0)
    si = lax.broadcasted_iota(I32, (L, L), 1)
    mask = (si >= ti) if reverse else (si <= ti)
    log_d = jnp.where(mask, b_col + a_row, -jnp.inf)
    mx = jnp.max(log_d, axis=1, keepdims=True)
    if state is None:
        m_old = jnp.zeros((1, 1), F32)
    else:
        c_old, n_old, m_old = state
    log_inter = b_col + m_old
    m_t = jnp.maximum(log_inter, mx)
    s = lax.dot_general(qb, kb, NT_DIMS, preferred_element_type=F32) * jnp.exp(log_d - m_t)
    num = jnp.dot(s.astype(BF16), vb, preferred_element_type=F32)
    den = jnp.sum(s, axis=1, keepdims=True)
    if state is not None:
        w_inter = jnp.exp(log_inter - m_t)
        num = num + w_inter * jnp.dot(qb, c_old.astype(BF16), preferred_element_type=F32)
        den = den + w_inter * jnp.sum(qf * n_old, axis=1, keepdims=True)
    h = num / jnp.maximum(jnp.abs(den), jnp.exp(-m_t))
    b_last = b_col[0:1, :] if reverse else b_col[L - 1:L, :]
    log_w = b_last - b_col + ig_col
    m_new = jnp.maximum(b_last + m_old, jnp.max(log_w, axis=0, keepdims=True))
    w_s = jnp.exp(log_w - m_new)
    kw = kf * w_s
    c_new = lax.dot_general(kw.astype(BF16), vb, TN_DIMS, preferred_element_type=F32)
    n_new = jnp.sum(kw, axis=0, keepdims=True)
    if state is not None:
        decay = jnp.exp(b_last + m_old - m_new)
        c_new = c_new + decay * c_old
        n_new = n_new + decay * n_old
    return h, (c_new, n_new, m_new)


def _mlstm_prompt_kernel(q_ref, k_ref, v_ref, gc_ref, rowf_ref, rowb_ref,
                         h_ref, c_ref, n_ref, m_ref, *, scale):
    hd = pl.program_id(1)
    qf = q_ref[...]
    kf = k_ref[...] * scale
    vb = v_ref[...].astype(BF16)
    gc = gc_ref[...]
    total = None
    for d, row_ref in ((0, rowf_ref), (1, rowb_ref)):
        j = d * 8 + hd
        hh, (c_new, n_new, m_new) = _mlstm_chunk(
            qf, kf, vb, _lane_pick(gc, j * 8 + 1), _lane_pick(gc, j * 8 + 2),
            row_ref[0:1, :], None, reverse=(d == 1))
        total = hh if total is None else total + hh
        c_ref[d] = c_new
        n_ref[d, pl.ds(hd, 1), :] = n_new
        m_ref[d] = m_new
    h_ref[...] = total


def mlstm_prompt(z, gcol, grow3, geo, heads, dk, dv):
    L = geo.tp
    kq = heads * dk // dk
    vo = 2 * heads * dk // dv
    return pl.pallas_call(
        functools.partial(_mlstm_prompt_kernel, scale=dk ** -0.5),
        grid=(geo.bp, heads),
        in_specs=[
            pl.BlockSpec((L, dk), lambda b, h: (b, h)),
            pl.BlockSpec((L, dk), lambda b, h: (b, kq + h)),
            pl.BlockSpec((L, dv), lambda b, h: (b, vo + h)),
            pl.BlockSpec((L, V7X_LANES), lambda b, h: (b, 0)),
            pl.BlockSpec((None, V7X_SUBLANES, L), lambda b, h: (h, 0, b)),
            pl.BlockSpec((None, V7X_SUBLANES, L), lambda b, h: (8 + h, 0, b)),
        ],
        out_specs=[
            pl.BlockSpec((L, dv), lambda b, h: (b, h)),
            pl.BlockSpec((None, 2, None, dk, dv), lambda b, h: (b, 0, h, 0, 0)),
            pl.BlockSpec((None, 2, heads, dk), lambda b, h: (b, 0, 0, 0)),
            pl.BlockSpec((None, 2, None, 1, 1), lambda b, h: (b, 0, h, 0, 0)),
        ],
        out_shape=[
            jax.ShapeDtypeStruct((geo.n, heads * dv), F32),
            jax.ShapeDtypeStruct((geo.bp, 2, heads, dk, dv), F32),
            jax.ShapeDtypeStruct((geo.bp, 2, heads, dk), F32),
            jax.ShapeDtypeStruct((geo.bp, 2, heads, 1, 1), F32),
        ],
        compiler_params=_cparams(("arbitrary", "arbitrary")),
    )(z, z, z, gcol, grow3, grow3)


def _mlstm_sample_kernel(*refs, scale, reverse, accumulate):
    (q_ref, k_ref, v_ref, gc_ref, row_ref, c0_ref, n0_ref, m0_ref, hp_ref,
     h_ref, c_sc, n_sc, m_sc) = refs
    hd = pl.program_id(1)
    c = pl.program_id(2)

    @pl.when(c == 0)
    def _():
        c_sc[...] = c0_ref[...]
        n_sc[...] = n0_ref[pl.ds(hd, 1), :]
        m_sc[...] = m0_ref[...]

    j = (8 if reverse else 0) + hd
    gc = gc_ref[...]
    hh, (c_new, n_new, m_new) = _mlstm_chunk(
        q_ref[...], k_ref[...] * scale, v_ref[...].astype(BF16),
        _lane_pick(gc, j * 8 + 1), _lane_pick(gc, j * 8 + 2), row_ref[0:1, :],
        (c_sc[...], n_sc[...], m_sc[...]), reverse)
    c_sc[...] = c_new
    n_sc[...] = n_new
    m_sc[...] = m_new
    if accumulate:
        h_ref[...] = hp_ref[...] + hh
    else:
        h_ref[...] = hh


def mlstm_sample(z, gcol, grow3, state_c, state_n, state_m5, jl, geo, heads, dk, dv, L,
                 reverse, hbuf, accumulate):
    nc = geo.ts // L
    off = geo.np_ // L
    kq = heads
    vo = 2 * heads * dk // dv
    d = 1 if reverse else 0

    def rb(b, c):
        return off + b * nc + ((nc - 1 - c) if reverse else c)

    in_specs = [
        pl.BlockSpec((L, dk), lambda b, h, c: (rb(b, c), h)),
        pl.BlockSpec((L, dk), lambda b, h, c: (rb(b, c), kq + h)),
        pl.BlockSpec((L, dv), lambda b, h, c: (rb(b, c), vo + h)),
        pl.BlockSpec((L, V7X_LANES), lambda b, h, c: (rb(b, c), 0)),
        pl.BlockSpec((None, V7X_SUBLANES, L), lambda b, h, c: (d * 8 + h, 0, rb(b, c))),
        pl.BlockSpec((None, None, None, None, dk, dv), lambda b, h, c: (b, jl, d, h, 0, 0)),
        pl.BlockSpec((None, None, None, heads, dk), lambda b, h, c: (b, jl, d, 0, 0)),
        pl.BlockSpec((None, None, None, None, 1, 1), lambda b, h, c: (b, jl, d, h, 0, 0)),
    ]
    args = [z, z, z, gcol, grow3, state_c, state_n, state_m5]
    hspec = pl.BlockSpec((L, dv), lambda b, h, c: (rb(b, c), h))
    in_specs.append(hspec if accumulate else pl.BlockSpec(memory_space=pl.ANY))
    args.append(hbuf)
    return pl.pallas_call(
        functools.partial(_mlstm_sample_kernel, scale=dk ** -0.5, reverse=reverse,
                          accumulate=accumulate),
        grid=(geo.bs, heads, nc),
        in_specs=in_specs,
        out_specs=hspec,
        out_shape=jax.ShapeDtypeStruct(hbuf.shape, F32),
        scratch_shapes=[pltpu.VMEM((dk, dv), F32), pltpu.VMEM((1, dk), F32), pltpu.VMEM((1, 1), F32)],
        input_output_aliases={len(args) - 1: 0},
        compiler_params=_cparams(("arbitrary", "arbitrary", "arbitrary")),
    )(*args)


def _mlstm_out_kernel(h_ref, o_ref, g_ref, out_ref, *, heads, dv):
    for hd in range(heads):
        sl = slice(hd * dv, (hd + 1) * dv)
        y = _rms(h_ref[:, sl]) * g_ref[hd:hd + 1, :]
        out_ref[:, sl] = (y * jax.nn.sigmoid(o_ref[:, sl])).astype(BF16)


def mlstm_out(hsum, z, g_head3, jl, heads, dv, tm=256):
    n, d = hsum.shape
    ocol = (z.shape[1] - d) // d
    return pl.pallas_call(
        functools.partial(_mlstm_out_kernel, heads=heads, dv=dv),
        grid=(n // tm,),
        in_specs=[
            pl.BlockSpec((tm, d), lambda i: (i, 0)),
            pl.BlockSpec((tm, d), lambda i: (i, ocol)),
            pl.BlockSpec((None, heads, dv), lambda i: (jl, 0, 0)),
        ],
        out_specs=pl.BlockSpec((tm, d), lambda i: (i, 0)),
        out_shape=jax.ShapeDtypeStruct((n, d), BF16),
        compiler_params=_cparams(("arbitrary",)),
    )(hsum, z, g_head3)


def _rope_kernel(q_ref, k_ref, c_ref, s_ref, qo_ref, ko_ref):
    cos = c_ref[...]
    sin = s_ref[...]
    lane = lax.broadcasted_iota(I32, cos.shape, 1)
    low = (lane & 32) == 0
    for ref, oref in ((q_ref, qo_ref), (k_ref, ko_ref)):
        for g in range(ref.shape[1] // V7X_LANES):
            sl = slice(g * V7X_LANES, (g + 1) * V7X_LANES)
            x = ref[:, sl]
            partner = jnp.where(low, pltpu.roll(x, V7X_LANES - 32, 1), pltpu.roll(x, 32, 1))
            oref[:, sl] = (x * cos + partner * sin).astype(BF16)


def rope_tables(ts, dh):
    half = dh // 2
    t = jnp.arange(ts)
    row = (t // GRID_W).astype(F32)
    col = (t % GRID_W).astype(F32)
    inv = ROPE_THETA ** (-jnp.arange(0, half, 2, dtype=F32) / half)
    ar = row[:, None] * inv
    ac = col[:, None] * inv
    cos = jnp.concatenate([jnp.cos(ar), jnp.cos(ar), jnp.cos(ac), jnp.cos(ac)], axis=-1)
    sin = jnp.concatenate([-jnp.sin(ar), jnp.sin(ar), -jnp.sin(ac), jnp.sin(ac)], axis=-1)
    return cos, sin


def rope_sample(qkv, cos, sin, geo, d, tm):
    off = geo.np_ // tm
    per = geo.ts // tm
    return pl.pallas_call(
        _rope_kernel,
        grid=(geo.ns // tm,),
        in_specs=[
            pl.BlockSpec((tm, d), lambda i: (off + i, 0)),
            pl.BlockSpec((tm, d), lambda i: (off + i, 1)),
            pl.BlockSpec((tm, V7X_LANES), lambda i: (i % per, 0)),
            pl.BlockSpec((tm, V7X_LANES), lambda i: (i % per, 0)),
        ],
        out_specs=[pl.BlockSpec((tm, d), lambda i: (i, 0)),
                   pl.BlockSpec((tm, d), lambda i: (i, 0))],
        out_shape=[jax.ShapeDtypeStruct((geo.ns, d), BF16),
                   jax.ShapeDtypeStruct((geo.ns, d), BF16)],
        compiler_params=_cparams(("arbitrary",)),
    )(qkv, qkv, cos, sin)


def _attn_kernel(*refs, dh, lam_init, cached):
    if cached:
        lam_ref, q_ref, k_ref, v_ref, kc_ref, vc_ref, g_ref, _, o_ref = refs
    else:
        lam_ref, q_ref, k_ref, v_ref, g_ref, o_ref = refs
    lp = lam_ref[...]
    lam = (jnp.exp(jnp.sum(lp[0:1] * lp[1:2], axis=1, keepdims=True))
           - jnp.exp(jnp.sum(lp[2:3] * lp[3:4], axis=1, keepdims=True)) + lam_init)
    scale = dh ** -0.5
    vb = v_ref[...].astype(BF16)
    if cached:
        vcb = vc_ref[...].astype(BF16)
    outs = []
    for c in range(2):
        sl = slice(c * dh, (c + 1) * dh)
        q = q_ref[:, sl].astype(BF16)
        s1 = lax.dot_general(q, k_ref[:, sl].astype(BF16), NT_DIMS,
                             preferred_element_type=F32) * scale
        m = jnp.max(s1, axis=1, keepdims=True)
        if cached:
            s2 = lax.dot_general(q, kc_ref[:, sl].astype(BF16), NT_DIMS,
                                 preferred_element_type=F32) * scale
            m = jnp.maximum(m, jnp.max(s2, axis=1, keepdims=True))
        e1 = jnp.exp(s1 - m)
        l = jnp.sum(e1, axis=1, keepdims=True)
        if cached:
            e2 = jnp.exp(s2 - m)
            l = l + jnp.sum(e2, axis=1, keepdims=True)
        inv = 1.0 / l
        o = jnp.dot((e1 * inv).astype(BF16), vb, preferred_element_type=F32)
        if cached:
            o = o + jnp.dot((e2 * inv).astype(BF16), vcb, preferred_element_type=F32)
        outs.append(o)
    o = outs[0] - lam * outs[1]
    o_ref[...] = (_rms(o) * g_ref[...] * (1.0 - lam_init)).astype(BF16)


def attn_prompt(qkv, a_lam, g_sub4, jl, geo, heads, dh, lam_init, out_rows):
    t = geo.tp
    w = 2 * dh
    return pl.pallas_call(
        functools.partial(_attn_kernel, dh=dh, lam_init=lam_init, cached=False),
        grid=(geo.bp, heads),
        in_specs=[
            pl.BlockSpec((None, 4, dh), lambda b, h: (jl, 0, 0)),
            pl.BlockSpec((t, w), lambda b, h: (b, h)),
            pl.BlockSpec((t, w), lambda b, h: (b, heads + h)),
            pl.BlockSpec((t, w), lambda b, h: (b, 2 * heads + h)),
            pl.BlockSpec((None, None, 1, w), lambda b, h: (jl, h, 0, 0)),
        ],
        out_specs=pl.BlockSpec((t, w), lambda b, h: (b, h)),
        out_shape=jax.ShapeDtypeStruct((out_rows, heads * w), BF16),
        compiler_params=_cparams(("arbitrary", "arbitrary")),
    )(a_lam, qkv, qkv, qkv, g_sub4)


def attn_sample(q_rot, k_rot, qkv, cache_k4, cache_v4, a_lam, g_sub4, jl, geo, heads, dh,
                lam_init, prev, tq):
    w = 2 * dh
    nq = geo.ts // tq
    offq = geo.np_ // tq
    offs = geo.np_ // geo.ts
    past = cache_k4.shape[2]
    return pl.pallas_call(
        functools.partial(_attn_kernel, dh=dh, lam_init=lam_init, cached=True),
        grid=(geo.bs, heads, nq),
        in_specs=[
            pl.BlockSpec((None, 4, dh), lambda b, h, i: (jl, 0, 0)),
            pl.BlockSpec((tq, w), lambda b, h, i: (b * nq + i, h)),
            pl.BlockSpec((geo.ts, w), lambda b, h, i: (b, h)),
            pl.BlockSpec((geo.ts, w), lambda b, h, i: (offs + b, 2 * heads + h)),
            pl.BlockSpec((None, None, past, w), lambda b, h, i: (b, jl, 0, h)),
            pl.BlockSpec((None, None, past, w), lambda b, h, i: (b, jl, 0, h)),
            pl.BlockSpec((None, None, 1, w), lambda b, h, i: (jl, h, 0, 0)),
            pl.BlockSpec(memory_space=pl.ANY),
        ],
        out_specs=pl.BlockSpec((tq, w), lambda b, h, i: (offq + b * nq + i, h)),
        out_shape=jax.ShapeDtypeStruct(prev.shape, BF16),
        input_output_aliases={7: 0},
        compiler_params=_cparams(("arbitrary", "arbitrary", "arbitrary")),
    )(a_lam, q_rot, k_rot, qkv, cache_k4, cache_v4, g_sub4, prev)


def _router_kernel(x_ref, g_ref, sh_ref, sc_ref, rwh_ref, rwl_ref, rb_ref,
                   h_ref, ei_ref, rk_ref, gwc_ref, cnt_ref, carry_sc, *, n_exp):
    i = pl.program_id(0)

    @pl.when(i == 0)
    def _():
        carry_sc[...] = jnp.zeros_like(carry_sc)

    y = _rms(x_ref[...]) * g_ref[...]
    hm = y * (1.0 + sc_ref[...]) + sh_ref[...]
    h_ref[...] = hm
    tm = hm.shape[0]
    hi = hm.astype(BF16)
    lo = (hm - hi.astype(F32)).astype(BF16)
    rwh = rwh_ref[...]
    logits = (lax.dot_general(rwh, hi, NT_DIMS, preferred_element_type=F32)
              + lax.dot_general(rwh, lo, NT_DIMS, preferred_element_type=F32)
              + lax.dot_general(rwl_ref[...], hi, NT_DIMS, preferred_element_type=F32))
    ex = jnp.exp(logits - jnp.max(logits, axis=0, keepdims=True))
    probs = ex / jnp.sum(ex, axis=0, keepdims=True)
    sel = probs + rb_ref[...]
    epg = n_exp // N_GROUPS
    sub = lax.broadcasted_iota(I32, (epg, tm), 0).astype(F32)
    ninf = -jnp.inf

    def top2(v):
        m1 = jnp.max(v, axis=0, keepdims=True)
        a1 = jnp.min(jnp.where(v == m1, sub, float(epg)), axis=0, keepdims=True)
        v2 = jnp.where(sub == a1, ninf, v)
        m2 = jnp.max(v2, axis=0, keepdims=True)
        a2 = jnp.min(jnp.where(v2 == m2, sub, float(epg)), axis=0, keepdims=True)
        return m1, a1, m2, a2

    best = None
    gi = None
    for g in range(N_GROUPS):
        m1, _, m2, _ = top2(sel[g * epg:(g + 1) * epg, :])
        score = m1 + m2
        if g == 0:
            best, gi = score, jnp.zeros((1, tm), F32)
        else:
            upd = score > best
            best = jnp.where(upd, score, best)
            gi = jnp.where(upd, float(g), gi)
    ing = jnp.zeros((epg, tm), F32)
    pin = jnp.zeros((epg, tm), F32)
    for g in range(N_GROUPS):
        pick = gi == float(g)
        ing = jnp.where(pick, sel[g * epg:(g + 1) * epg, :], ing)
        pin = jnp.where(pick, probs[g * epg:(g + 1) * epg, :], pin)
    _, l1, _, l2 = top2(ing)
    w1 = jnp.sum(jnp.where(sub == l1, pin, 0.0), axis=0, keepdims=True)
    w2 = jnp.sum(jnp.where(sub == l2, pin, 0.0), axis=0, keepdims=True)
    ws = w1 + w2
    e1 = (gi * epg + l1).astype(I32)
    e2 = (gi * epg + l2).astype(I32)

    eio = lax.broadcasted_iota(I32, (n_exp, tm), 0)
    is1 = eio == e1
    is2 = eio == e2
    mem = jnp.where(is1 | is2, 1.0, 0.0)
    tri = jnp.where(lax.broadcasted_iota(I32, (tm, tm), 0) < lax.broadcasted_iota(I32, (tm, tm), 1),
                    1.0, 0.0).astype(BF16)
    pre = jnp.dot(mem.astype(BF16), tri, preferred_element_type=F32) + carry_sc[...]
    r1 = jnp.sum(jnp.where(is1, pre, 0.0), axis=0, keepdims=True)
    r2 = jnp.sum(jnp.where(is2, pre, 0.0), axis=0, keepdims=True)
    carry = carry_sc[...] + jnp.sum(mem, axis=1, keepdims=True)
    carry_sc[...] = carry
    cnt_ref[...] = jnp.broadcast_to(carry, cnt_ref.shape)

    row8 = lax.broadcasted_iota(I32, (V7X_SUBLANES, tm), 0)
    ei_ref[...] = jnp.where(row8 == 0, e1, jnp.where(row8 == 1, e2, 0))
    rk_ref[...] = jnp.where(row8 == 0, r1.astype(I32), jnp.where(row8 == 1, r2.astype(I32), 0))
    row128 = lax.broadcasted_iota(I32, (V7X_LANES, tm), 0)
    gw = jnp.where(row128 == 0, w1 / ws, jnp.where(row128 == 1, w2 / ws, 0.0))
    gwc_ref[...] = gw.T


def moe_router(x, g_norm4, mod, rwt_hi, rwt_lo, rb_col, geo, layer, tm=256):
    n, d = x.shape
    n_exp = rwt_hi.shape[0]
    return pl.pallas_call(
        functools.partial(_router_kernel, n_exp=n_exp),
        grid=(n // tm,),
        in_specs=[
            pl.BlockSpec((tm, d), lambda i: (i, 0)),
            pl.BlockSpec((None, None, 1, d), lambda i: (layer, 1, 0, 0)),
            pl.BlockSpec((None, None, None, 1, d), lambda i: (layer, geo.mod_row(i, tm), 3, 0, 0)),
            pl.BlockSpec((None, None, None, 1, d), lambda i: (layer, geo.mod_row(i, tm), 4, 0, 0)),
            pl.BlockSpec((n_exp, d), lambda i: (0, 0)),
            pl.BlockSpec((n_exp, d), lambda i: (0, 0)),
            pl.BlockSpec((n_exp, 1), lambda i: (0, 0)),
        ],
        out_specs=[
            pl.BlockSpec((tm, d), lambda i: (i, 0)),
            pl.BlockSpec((V7X_SUBLANES, tm), lambda i: (0, i)),
            pl.BlockSpec((V7X_SUBLANES, tm), lambda i: (0, i)),
            pl.BlockSpec((tm, V7X_LANES), lambda i: (i, 0)),
            pl.BlockSpec((n_exp, V7X_LANES), lambda i: (0, 0)),
        ],
        out_shape=[
            jax.ShapeDtypeStruct((n, d), F32),
            jax.ShapeDtypeStruct((V7X_SUBLANES, n), I32),
            jax.ShapeDtypeStruct((V7X_SUBLANES, n), I32),
            jax.ShapeDtypeStruct((n, V7X_LANES), F32),
            jax.ShapeDtypeStruct((n_exp, V7X_LANES), F32),
        ],
        scratch_shapes=[pltpu.VMEM((n_exp, 1), F32)],
        compiler_params=_cparams(("arbitrary",)),
    )(x, g_norm4, mod, mod, rwt_hi, rwt_lo, rb_col)


def _gather_kernel(tok_ref, nused_ref, h_hbm, o_ref, buf, sem, *, bm):
    blk = pl.program_id(0)

    @pl.when(blk < nused_ref[0])
    def _():
        def row_copy(r):
            return pltpu.make_async_copy(h_hbm.at[pl.ds(tok_ref[blk * bm + r], 1)],
                                         buf.at[pl.ds(r, 1)], sem)

        def start(r, carry):
            row_copy(r).start()
            return carry

        def wait(r, carry):
            row_copy(r).wait()
            return carry

        lax.fori_loop(0, bm, start, 0)
        lax.fori_loop(0, bm, wait, 0)
        o_ref[...] = buf[...].astype(BF16)


def moe_gather(h2, slot_tok, n_used, n_blocks, bm):
    n, d = h2.shape
    return pl.pallas_call(
        functools.partial(_gather_kernel, bm=bm),
        grid_spec=pltpu.PrefetchScalarGridSpec(
            num_scalar_prefetch=2,
            grid=(n_blocks,),
            in_specs=[pl.BlockSpec(memory_space=pl.ANY)],
            out_specs=pl.BlockSpec((bm, d), lambda b, tok, nu: (b, 0)),
            scratch_shapes=[pltpu.VMEM((bm, d), F32), pltpu.SemaphoreType.DMA],
        ),
        out_shape=jax.ShapeDtypeStruct((n_blocks * bm, d), BF16),
        compiler_params=_cparams(("arbitrary",)),
    )(slot_tok, n_used, h2)


def _moe_up_kernel(blk_ref, f_ref, e_ref, first_ref, valid_ref,
                   x_ref, wg_ref, wu_ref, a_ref, wg_sc, wu_sc):
    w = pl.program_id(0)

    @pl.when(valid_ref[w] == 1)
    def _():
        @pl.when(first_ref[w] == 1)
        def _():
            wg_sc[...] = wg_ref[...].astype(BF16)
            wu_sc[...] = wu_ref[...].astype(BF16)

        x = x_ref[...]
        g = jnp.dot(x, wg_sc[...], preferred_element_type=F32)
        u = jnp.dot(x, wu_sc[...], preferred_element_type=F32)
        a_ref[...] = (_silu(g) * u).astype(BF16)


def moe_up(xs, tabs, w_gate, w_up, layer, bm, tf):
    n_slots, d = xs.shape
    ff = w_gate.shape[3]
    nw = tabs[0].shape[0]
    wspec = pl.BlockSpec((None, None, d, tf), lambda w, blk, f, e, fi, va: (layer, e[w], 0, f[w]))
    return pl.pallas_call(
        _moe_up_kernel,
        grid_spec=pltpu.PrefetchScalarGridSpec(
            num_scalar_prefetch=5,
            grid=(nw,),
            in_specs=[pl.BlockSpec((bm, d), lambda w, blk, f, e, fi, va: (blk[w], 0)), wspec, wspec],
            out_specs=pl.BlockSpec((bm, tf), lambda w, blk, f, e, fi, va: (blk[w], f[w])),
            scratch_shapes=[pltpu.VMEM((d, tf), BF16), pltpu.VMEM((d, tf), BF16)],
        ),
        out_shape=jax.ShapeDtypeStruct((n_slots, ff), BF16),
        compiler_params=_cparams(("arbitrary",)),
    )(*tabs, xs, w_gate, w_up)


def _moe_down_kernel(blk_ref, f_ref, e_ref, first_ref, valid_ref, a_ref, wd_ref, y_ref, wd_sc):
    w = pl.program_id(0)

    @pl.when(valid_ref[w] == 1)
    def _():
        @pl.when(first_ref[w] == 1)
        def _():
            wd_sc[...] = wd_ref[...].astype(BF16)

        y_ref[...] = jnp.dot(a_ref[...], wd_sc[...], preferred_element_type=F32)


def moe_down(act, tabs, w_down, layer, bm, tn):
    n_slots, ff = act.shape
    d = w_down.shape[3]
    nw = tabs[0].shape[0]
    return pl.pallas_call(
        _moe_down_kernel,
        grid_spec=pltpu.PrefetchScalarGridSpec(
            num_scalar_prefetch=5,
            grid=(nw,),
            in_specs=[
                pl.BlockSpec((bm, ff), lambda w, blk, f, e, fi, va: (blk[w], 0)),
                pl.BlockSpec((None, None, ff, tn), lambda w, blk, f, e, fi, va: (layer, e[w], 0, f[w])),
            ],
            out_specs=pl.BlockSpec((bm, tn), lambda w, blk, f, e, fi, va: (blk[w], f[w])),
            scratch_shapes=[pltpu.VMEM((ff, tn), BF16)],
        ),
        out_shape=jax.ShapeDtypeStruct((n_slots, d), F32),
        compiler_params=_cparams(("arbitrary",)),
    )(*tabs, act, w_down)


def _work_tables(nb_e, n_inner, n_blocks):
    n_exp = nb_e.shape[0]
    s_e = jnp.cumsum(nb_e) - nb_e
    tot = jnp.sum(nb_e)
    wstart = n_inner * s_e
    wend = wstart + n_inner * nb_e
    nw = n_inner * n_blocks
    n_valid = n_inner * tot
    w = jnp.minimum(jnp.arange(nw, dtype=I32), n_valid - 1)
    e = jnp.clip(jnp.searchsorted(wend, w, side='right'), 0, n_exp - 1).astype(I32)
    r = w - wstart[e]
    nb = jnp.maximum(nb_e[e], 1)
    f = r // nb
    p = r % nb
    blk = s_e[e] + p
    valid = (jnp.arange(nw, dtype=I32) < n_valid).astype(I32)
    first = valid * (p == 0).astype(I32)
    return (blk.astype(I32), f.astype(I32), e, first, valid)


def _combine_kernel(dest_ref, x_ref, g_ref, gwc_ref, y_hbm, o_ref, buf, sem, *, tb, n_tok):
    i = pl.program_id(0)

    def row_copy(k, r):
        return pltpu.make_async_copy(y_hbm.at[pl.ds(dest_ref[k * n_tok + i * tb + r], 1)],
                                     buf.at[k, pl.ds(r, 1)], sem)

    def start(r, carry):
        row_copy(0, r).start()
        row_copy(1, r).start()
        return carry

    def wait(r, carry):
        row_copy(0, r).wait()
        row_copy(1, r).wait()
        return carry

    lax.fori_loop(0, tb, start, 0)
    lax.fori_loop(0, tb, wait, 0)
    gwc = gwc_ref[...]
    y = gwc[:, 0:1] * buf[0] + gwc[:, 1:2] * buf[1]
    o_ref[...] = x_ref[...] + g_ref[...] * y


def moe_combine(x, yb, dest, gwc, mod, geo, layer, tb=256):
    n, d = x.shape
    return pl.pallas_call(
        functools.partial(_combine_kernel, tb=tb, n_tok=n),
        grid_spec=pltpu.PrefetchScalarGridSpec(
            num_scalar_prefetch=1,
            grid=(n // tb,),
            in_specs=[
                pl.BlockSpec((tb, d), lambda i, dr: (i, 0)),
                pl.BlockSpec((None, None, None, 1, d), lambda i, dr: (layer, geo.mod_row(i, tb), 5, 0, 0)),
                pl.BlockSpec((tb, V7X_LANES), lambda i, dr: (i, 0)),
                pl.BlockSpec(memory_space=pl.ANY),
            ],
            out_specs=pl.BlockSpec((tb, d), lambda i, dr: (i, 0)),
            scratch_shapes=[pltpu.VMEM((2, tb, d), F32), pltpu.SemaphoreType.DMA],
        ),
        out_shape=jax.ShapeDtypeStruct((n, d), F32),
        compiler_params=_cparams(("arbitrary",)),
    )(dest, x, mod, gwc, yb)


def moe_layer(x, g_norm4, mod, rwt_hi, rwt_lo, rb_col, w_gate, w_up, w_down, geo, layer):
    n, d = x.shape
    n_exp = rwt_hi.shape[0]
    bm = MOE_BM
    h2, ei, rk, gwc, cnt = moe_router(x, g_norm4, mod, rwt_hi, rwt_lo, rb_col, geo, layer)
    counts = cnt[:, 0].astype(I32)
    nb_e = (counts + bm - 1) // bm
    pad_start = (jnp.cumsum(nb_e) - nb_e) * bm
    e_flat = ei[:2].reshape(-1)
    dest = pad_start[e_flat] + rk[:2].reshape(-1)
    n_blocks = (2 * n) // bm + n_exp
    tok = jnp.tile(jnp.arange(n, dtype=I32), 2)
    slot_tok = jnp.zeros((n_blocks * bm,), I32).at[dest].set(tok)
    n_used = jnp.sum(nb_e).astype(I32).reshape(1)
    xs = moe_gather(h2, slot_tok, n_used, n_blocks, bm)
    ff = w_gate.shape[3]
    act = moe_up(xs, _work_tables(nb_e, ff // MOE_TF, n_blocks), w_gate, w_up, layer, bm, MOE_TF)
    yb = moe_down(act, _work_tables(nb_e, d // MOE_TN, n_blocks), w_down, layer, bm, MOE_TN)
    return moe_combine(x, yb, dest, gwc, mod, geo, layer)


def _final_kernel(x_ref, g_ref, o_ref):
    o_ref[...] = _rms(x_ref[...]) * g_ref[...]


def final_norm(x, g_row, row0, rows, tm=256):
    d = x.shape[1]
    off = row0 // tm
    return pl.pallas_call(
        _final_kernel,
        grid=(rows // tm,),
        in_specs=[pl.BlockSpec((tm, d), lambda i: (off + i, 0)),
                  pl.BlockSpec((1, d), lambda i: (0, 0))],
        out_specs=pl.BlockSpec((tm, d), lambda i: (i, 0)),
        out_shape=jax.ShapeDtypeStruct((rows, d), F32),
        compiler_params=_cparams(("arbitrary",)),
    )(x, g_row)


def _gate_weights(w_in, b_gate, n_main, heads):
    j = jnp.arange(V7X_LANES) // V7X_SUBLANES
    d_, h_ = j // heads, j % heads
    ci = n_main + d_ * 2 * heads + h_
    cf = ci + heads
    wi = w_in[:, ci].astype(BF16)
    wf = w_in[:, cf].astype(BF16)
    bi = b_gate[ci - n_main].reshape(1, -1)
    bf = b_gate[cf - n_main].reshape(1, -1)
    return wi, wf, bi, bf


def kernel(x_prompt, x_sample, state_C, state_n, state_m, cache_k, cache_v, c, c_ctx,
           w_mod, b_mod, g_norm, m_w_in, m_b_gate, m_g_head, m_w_out,
           a_w_qkv, a_lam, a_g_sub, a_w_o, router_w, router_b, w_gate, w_up, w_down, g_final):
    bp, tp, d = x_prompt.shape
    bs, ts, _ = x_sample.shape
    depth = w_mod.shape[0]
    geo = Geo(bp, tp, bs, ts)
    tm = geo.tm
    m_heads, m_dv = m_g_head.shape[1], m_g_head.shape[2]
    m_dk = m_dv // 2
    a_heads = a_g_sub.shape[1]
    a_dh = a_g_sub.shape[2] // 2
    n_exp = router_w.shape[1]
    assert tp == MLSTM_CHUNK and ts % MLSTM_CHUNK == 0 and geo.np_ % ts == 0
    assert 2 * m_heads * V7X_SUBLANES == V7X_LANES and n_exp // N_GROUPS == V7X_SUBLANES

    x = jnp.concatenate([x_prompt.reshape(geo.np_, d), x_sample.reshape(geo.ns, d)], axis=0)

    n_rows = 2 * V7X_SUBLANES
    cv = jnp.zeros((n_rows, d), F32).at[0].set(c_ctx).at[1:1 + bs].set(c)
    mod = adaln_all(cv, w_mod, b_mod).reshape(depth, n_rows, 6, 1, d)
    g_norm4 = g_norm.reshape(depth, 2, 1, d)

    rwt = router_w.T
    rwt_hi = rwt.astype(BF16)
    rwt_lo = (rwt - rwt_hi.astype(F32)).astype(BF16)
    rb_col = router_b.reshape(n_exp, 1)

    new_c, new_n, new_m, new_k, new_v = [], [], [], [], []
    for i in range(depth):
        jl = i // N_MIXERS
        h = modnorm(x, g_norm4, mod, geo, i, 0, 0, 1)
        if i % N_MIXERS == 0:
            n_main = 2 * m_heads * m_dk + 2 * m_heads * m_dv
            z = matmul(h, m_w_in, jl, n_main, tm)
            wi, wf, bi, bf = _gate_weights(m_w_in[jl], m_b_gate[jl], n_main, m_heads)
            L = MLSTM_CHUNK
            gcol, grow = mlstm_gates(h, wi, wf, bi, bf, L, tm)
            grow3 = grow.reshape(2 * m_heads, V7X_SUBLANES, geo.n)
            hsum, c_new, n_new, m_new = mlstm_prompt(z, gcol, grow3, geo, m_heads, m_dk, m_dv)
            new_c.append(c_new)
            new_n.append(n_new)
            new_m.append(m_new.reshape(bp, 2, m_heads))
            state_m5 = state_m.reshape(state_m.shape + (1, 1))
            for reverse in (False, True):
                hsum = mlstm_sample(z, gcol, grow3, state_C, state_n, state_m5, jl, geo, m_heads,
                                    m_dk, m_dv, L, reverse, hsum, reverse)
            mix_in = mlstm_out(hsum, z, m_g_head, jl, m_heads, m_dv)
            x = matmul_residual(mix_in, m_w_out, jl, x, mod, geo, i, 2, tm)
        else:
            lam_init = 0.8 - 0.6 * math.exp(-0.3 * i)
            qkv = matmul(h, a_w_qkv, jl, 3 * d, tm)
            new_k.append(qkv[:geo.np_, d:2 * d].reshape(bp, tp, a_heads, 2, a_dh))
            new_v.append(qkv[:geo.np_, 2 * d:].reshape(bp, tp, a_heads, 2 * a_dh))
            g_sub4 = a_g_sub.reshape(a_g_sub.shape[0], a_heads, 1, 2 * a_dh)
            cos, sin = rope_tables(ts, a_dh)
            q_rot, k_rot = rope_sample(qkv, cos, sin, geo, d, min(512, ts))
            att = attn_prompt(qkv, a_lam, g_sub4, jl, geo, a_heads, a_dh, lam_init, geo.n)
            ck4 = cache_k.reshape(bs, cache_k.shape[1], cache_k.shape[2], d)
            cv4 = cache_v.reshape(bs, cache_v.shape[1], cache_v.shape[2], d)
            att = attn_sample(q_rot, k_rot, qkv, ck4, cv4, a_lam, g_sub4, jl, geo, a_heads, a_dh,
                              lam_init, att, min(ATTN_TQ, ts))
            x = matmul_residual(att, a_w_o, jl, x, mod, geo, i, 2, tm)
        x = moe_layer(x, g_norm4, mod, rwt_hi, rwt_lo, rb_col, w_gate, w_up, w_down, geo, i)

    g_row = g_final.reshape(1, d)
    y_prompt = final_norm(x, g_row, 0, geo.np_).reshape(bp, tp, d)
    y_sample = final_norm(x, g_row, geo.np_, geo.ns).reshape(bs, ts, d)
    return (y_prompt, y_sample,
            jnp.stack(new_c, axis=1), jnp.stack(new_n, axis=1), jnp.stack(new_m, axis=1),
            jnp.stack(new_k, axis=1), jnp.stack(new_v, axis=1))
```

```python
import functools
import math

import jax
import jax.numpy as jnp
from jax import lax
from jax.experimental import pallas as pl
from jax.experimental.pallas import tpu as pltpu

F32 = jnp.float32
BF16 = jnp.bfloat16
I32 = jnp.int32

EPS = 1e-6
N_GROUPS = 8
GRID_W = 64
ROPE_THETA = 10000.0
N_MIXERS = 2

V7X_LANES = 128
V7X_SUBLANES = 8
V7X_VMEM_LIMIT = 56 * 1024 * 1024

MLSTM_CHUNK = 256
MOE_BM = 256
MOE_TF = 512
ATTN_TQ = 256

NT_DIMS = (((1,), (1,)), ((), ()))
TN_DIMS = (((0,), (0,)), ((), ()))


def _cparams(sem):
    return pltpu.CompilerParams(dimension_semantics=sem, vmem_limit_bytes=V7X_VMEM_LIMIT)


def _silu(x):
    return x * jax.nn.sigmoid(x)


def _rms(x):
    return x * lax.rsqrt(jnp.mean(x * x, axis=-1, keepdims=True) + EPS)


def _adaln_kernel(cv_ref, w_ref, b_ref, o_ref):
    s = _silu(cv_ref[...]).astype(BF16)
    o_ref[...] = jnp.dot(s, w_ref[...].astype(BF16), preferred_element_type=F32) + b_ref[...]


def adaln_all(cv, w_mod, b_mod, tn=512):
    depth, d, n6 = w_mod.shape
    r = cv.shape[0]
    return pl.pallas_call(
        _adaln_kernel,
        grid=(depth, n6 // tn),
        in_specs=[
            pl.BlockSpec((r, d), lambda l, j: (0, 0)),
            pl.BlockSpec((None, d, tn), lambda l, j: (l, 0, j)),
            pl.BlockSpec((None, 1, tn), lambda l, j: (l, 0, j)),
        ],
        out_specs=pl.BlockSpec((None, r, tn), lambda l, j: (l, 0, j)),
        out_shape=jax.ShapeDtypeStruct((depth, r, n6), F32),
        compiler_params=_cparams(("arbitrary", "arbitrary")),
    )(cv, w_mod, b_mod.reshape(depth, 1, n6))


class Geo:
    def __init__(self, bp, tp, bs, ts):
        self.bp, self.tp, self.bs, self.ts = bp, tp, bs, ts
        self.np_ = bp * tp
        self.ns = bs * ts
        self.n = self.np_ + self.ns
        self.tm = min(1024, math.gcd(self.np_, ts))

    def mod_row(self, i, tm):
        r0 = i * tm
        return jnp.where(r0 < self.np_, 0, 1 + (r0 - self.np_) // self.ts)


def _modnorm_kernel(x_ref, g_ref, sh_ref, sc_ref, o_ref):
    y = _rms(x_ref[...]) * g_ref[...]
    o_ref[...] = (y * (1.0 + sc_ref[...]) + sh_ref[...]).astype(o_ref.dtype)


def modnorm(x, g_norm4, mod, geo, layer, which_g, which_sh, which_sc, tm=256):
    n, d = x.shape
    return pl.pallas_call(
        _modnorm_kernel,
        grid=(n // tm,),
        in_specs=[
            pl.BlockSpec((tm, d), lambda i: (i, 0)),
            pl.BlockSpec((None, None, 1, d), lambda i: (layer, which_g, 0, 0)),
            pl.BlockSpec((None, None, None, 1, d), lambda i: (layer, geo.mod_row(i, tm), which_sh, 0, 0)),
            pl.BlockSpec((None, None, None, 1, d), lambda i: (layer, geo.mod_row(i, tm), which_sc, 0, 0)),
        ],
        out_specs=pl.BlockSpec((tm, d), lambda i: (i, 0)),
        out_shape=jax.ShapeDtypeStruct((n, d), BF16),
        compiler_params=_cparams(("arbitrary",)),
    )(x, g_norm4, mod, mod)


def _mm_kernel(a_ref, w_ref, o_ref):
    o_ref[...] = jnp.dot(a_ref[...], w_ref[...].astype(BF16),
                         preferred_element_type=F32).astype(o_ref.dtype)


def _mm_res_kernel(a_ref, w_ref, x_ref, g_ref, o_ref):
    acc = jnp.dot(a_ref[...], w_ref[...].astype(BF16), preferred_element_type=F32)
    o_ref[...] = x_ref[...] + g_ref[...] * acc


def matmul(a, w3, widx, n_cols, tm, tn=512, out_dtype=F32):
    m, k = a.shape
    return pl.pallas_call(
        _mm_kernel,
        grid=(m // tm, n_cols // tn),
        in_specs=[
            pl.BlockSpec((tm, k), lambda i, j: (i, 0)),
            pl.BlockSpec((None, k, tn), lambda i, j: (widx, 0, j)),
        ],
        out_specs=pl.BlockSpec((tm, tn), lambda i, j: (i, j)),
        out_shape=jax.ShapeDtypeStruct((m, n_cols), out_dtype),
        compiler_params=_cparams(("arbitrary", "arbitrary")),
    )(a, w3)


def matmul_residual(a, w3, widx, x, mod, geo, layer, which_gate, tm, tn=512):
    m, k = a.shape
    d = w3.shape[2]
    return pl.pallas_call(
        _mm_res_kernel,
        grid=(m // tm, d // tn),
        in_specs=[
            pl.BlockSpec((tm, k), lambda i, j: (i, 0)),
            pl.BlockSpec((None, k, tn), lambda i, j: (widx, 0, j)),
            pl.BlockSpec((tm, tn), lambda i, j: (i, j)),
            pl.BlockSpec((None, None, None, 1, tn),
                         lambda i, j: (layer, geo.mod_row(i, tm), which_gate, 0, j)),
        ],
        out_specs=pl.BlockSpec((tm, tn), lambda i, j: (i, j)),
        out_shape=jax.ShapeDtypeStruct((m, d), F32),
        compiler_params=_cparams(("arbitrary", "arbitrary")),
    )(a, w3, x, mod)


def _gates_kernel(h_ref, wi_ref, wf_ref, bi_ref, bf_ref, col_ref, row_ref, *, chunk):
    h = h_ref[...]
    tm = h.shape[0]
    gi = jnp.dot(h, wi_ref[...], preferred_element_type=F32) + bi_ref[...]
    xf = jnp.dot(h, wf_ref[...], preferred_element_type=F32) + bf_ref[...]
    lf = jnp.minimum(xf, 0.0) - jnp.log1p(jnp.exp(-jnp.abs(xf)))
    r = lax.broadcasted_iota(I32, (tm, V7X_LANES), 0) % chunk
    lane = lax.broadcasted_iota(I32, (tm, V7X_LANES), 1)
    pre = lf
    suf = lf
    s = 1
    while s < chunk:
        pre = pre + jnp.where(r >= s, pltpu.roll(pre, s, 0), 0.0)
        suf = suf + jnp.where(r < chunk - s, pltpu.roll(suf, tm - s, 0), 0.0)
        s *= 2
    b = jnp.where(lane < V7X_LANES // 2, pre, suf)
    sub = lane % V7X_SUBLANES
    col = jnp.where(sub == 0, gi - b, jnp.where(sub == 1, b, gi))
    col_ref[...] = col
    for ci in range(tm // chunk):
        row_ref[ci] = col[ci * chunk:(ci + 1) * chunk, :].T


def mlstm_gates(h, wi, wf, bi, bf, chunk, tm):
    n, d = h.shape
    return pl.pallas_call(
        functools.partial(_gates_kernel, chunk=chunk),
        grid=(n // tm,),
        in_specs=[
            pl.BlockSpec((tm, d), lambda i: (i, 0)),
            pl.BlockSpec((d, V7X_LANES), lambda i: (0, 0)),
            pl.BlockSpec((d, V7X_LANES), lambda i: (0, 0)),
            pl.BlockSpec((1, V7X_LANES), lambda i: (0, 0)),
            pl.BlockSpec((1, V7X_LANES), lambda i: (0, 0)),
        ],
        out_specs=[
            pl.BlockSpec((tm, V7X_LANES), lambda i: (i, 0)),
            pl.BlockSpec((tm // chunk, V7X_LANES, chunk), lambda i: (i, 0, 0)),
        ],
        out_shape=[jax.ShapeDtypeStruct((n, V7X_LANES), F32),
                   jax.ShapeDtypeStruct((n // chunk, V7X_LANES, chunk), F32)],
        compiler_params=_cparams(("arbitrary",)),
    )(h, wi, wf, bi, bf)


def _lane_pick(gc, lane_idx):
    lane = lax.broadcasted_iota(I32, gc.shape, 1)
    return jnp.sum(jnp.where(lane == lane_idx, gc, 0.0), axis=1, keepdims=True)


def _mlstm_chunk(qf, kf, vb, b_col, ig_col, a_row, state, reverse):
    L = qf.shape[0]
    qb = qf.astype(BF16)
    kb = kf.astype(BF16)
    ti = lax.broadcasted_iota(I32, (L, L), 0)
    si = lax.broadcasted_iota(I32, (L, L), 1)
    mask = (si >= ti) if reverse else (si <= ti)
    log_d = jnp.where(mask, b_col + a_row, -jnp.inf)
    mx = jnp.max(log_d, axis=1, keepdims=True)
    if state is None:
        m_old = jnp.zeros((1, 1), F32)
    else:
        c_old, n_old, m_old = state
    log_inter = b_col + m_old
    m_t = jnp.maximum(log_inter, mx)
    s = lax.dot_general(qb, kb, NT_DIMS, preferred_element_type=F32) * jnp.exp(log_d - m_t)
    num = jnp.dot(s.astype(BF16), vb, preferred_element_type=F32)
    den = jnp.sum(s, axis=1, keepdims=True)
    if state is not None:
        w_inter = jnp.exp(log_inter - m_t)
        num = num + w_inter * jnp.dot(qb, c_old.astype(BF16), preferred_element_type=F32)
        den = den + w_inter * jnp.sum(qf * n_old, axis=1, keepdims=True)
    h = num / jnp.maximum(jnp.abs(den), jnp.exp(-m_t))
    b_last = b_col[0:1, :] if reverse else b_col[L - 1:L, :]
    log_w = b_last - b_col + ig_col
    m_new = jnp.maximum(b_last + m_old, jnp.max(log_w, axis=0, keepdims=True))
    w_s = jnp.exp(log_w - m_new)
    kw = kf * w_s
    c_new = lax.dot_general(kw.astype(BF16), vb, TN_DIMS, preferred_element_type=F32)
    n_new = jnp.sum(kw, axis=0, keepdims=True)
    if state is not None:
        decay = jnp.exp(b_last + m_old - m_new)
        c_new = c_new + decay * c_old
        n_new = n_new + decay * n_old
    return h, (c_new, n_new, m_new)


def _mlstm_prompt_kernel(q_ref, k_ref, v_ref, gc_ref, rowf_ref, rowb_ref,
                         h_ref, c_ref, n_ref, m_ref, *, scale):
    hd = pl.program_id(1)
    qf = q_ref[...].astype(F32)
    kf = k_ref[...].astype(F32) * scale
    vb = v_ref[...]
    gc = gc_ref[...]
    total = None
    for d, row_ref in ((0, rowf_ref), (1, rowb_ref)):
        j = d * 8 + hd
        hh, (c_new, n_new, m_new) = _mlstm_chunk(
            qf, kf, vb, _lane_pick(gc, j * 8 + 1), _lane_pick(gc, j * 8 + 2),
            row_ref[0:1, :], None, reverse=(d == 1))
        total = hh if total is None else total + hh
        c_ref[d] = c_new
        n_ref[d, pl.ds(hd, 1), :] = n_new
        m_ref[d] = m_new
    h_ref[...] = total


def mlstm_prompt(z, gcol, grow, geo, heads, dk, dv):
    L = geo.tp
    kq = heads
    vo = 2 * heads * dk // dv
    return pl.pallas_call(
        functools.partial(_mlstm_prompt_kernel, scale=dk ** -0.5),
        grid=(geo.bp, heads),
        in_specs=[
            pl.BlockSpec((L, dk), lambda b, h: (b, h)),
            pl.BlockSpec((L, dk), lambda b, h: (b, kq + h)),
            pl.BlockSpec((L, dv), lambda b, h: (b, vo + h)),
            pl.BlockSpec((L, V7X_LANES), lambda b, h: (b, 0)),
            pl.BlockSpec((None, V7X_SUBLANES, L), lambda b, h: (b, h, 0)),
            pl.BlockSpec((None, V7X_SUBLANES, L), lambda b, h: (b, heads + h, 0)),
        ],
        out_specs=[
            pl.BlockSpec((L, dv), lambda b, h: (b, h)),
            pl.BlockSpec((None, 2, None, dk, dv), lambda b, h: (b, 0, h, 0, 0)),
            pl.BlockSpec((None, 2, heads, dk), lambda b, h: (b, 0, 0, 0)),
            pl.BlockSpec((None, 2, None, 1, 1), lambda b, h: (b, 0, h, 0, 0)),
        ],
        out_shape=[
            jax.ShapeDtypeStruct((geo.n, heads * dv), F32),
            jax.ShapeDtypeStruct((geo.bp, 2, heads, dk, dv), F32),
            jax.ShapeDtypeStruct((geo.bp, 2, heads, dk), F32),
            jax.ShapeDtypeStruct((geo.bp, 2, heads, 1, 1), F32),
        ],
        compiler_params=_cparams(("arbitrary", "arbitrary")),
    )(z, z, z, gcol, grow, grow)


def _mlstm_sample_kernel(q_ref, k_ref, v_ref, gc_ref, rowf_ref, rowb_ref, c0_ref, n0_ref, m0_ref,
                         _, h_ref, c_sc, n_sc, m_sc, *, scale, L, nc, heads):
    hd = pl.program_id(1)
    for d, row_ref in ((0, rowf_ref), (1, rowb_ref)):
        c_sc[...] = c0_ref[d]
        n_sc[...] = n0_ref[d, pl.ds(hd, 1), :]
        m_sc[...] = m0_ref[d]
        j = d * heads + hd

        def body(i, carry, d=d, row_ref=row_ref, j=j):
            c = (nc - 1 - i) if d == 1 else i
            rows = pl.ds(pl.multiple_of(c * L, L), L)
            gc = gc_ref[rows, :]
            hh, (c_new, n_new, m_new) = _mlstm_chunk(
                q_ref[rows, :].astype(F32), k_ref[rows, :].astype(F32) * scale, v_ref[rows, :],
                _lane_pick(gc, j * 8 + 1), _lane_pick(gc, j * 8 + 2), row_ref[c, 0:1, :],
                (c_sc[...], n_sc[...], m_sc[...]), reverse=(d == 1))
            c_sc[...] = c_new
            n_sc[...] = n_new
            m_sc[...] = m_new
            if d == 0:
                h_ref[rows, :] = hh
            else:
                h_ref[rows, :] += hh
            return carry

        lax.fori_loop(0, nc, body, 0)


def mlstm_sample(z, gcol, grow, state_c, state_n, state_m5, jl, geo, heads, dk, dv, L, hbuf):
    nc = geo.ts // L
    offs = geo.np_ // geo.ts
    kq = heads
    vo = 2 * heads * dk // dv
    ts = geo.ts
    in_specs = [
        pl.BlockSpec((ts, dk), lambda b, h: (offs + b, h)),
        pl.BlockSpec((ts, dk), lambda b, h: (offs + b, kq + h)),
        pl.BlockSpec((ts, dv), lambda b, h: (offs + b, vo + h)),
        pl.BlockSpec((ts, V7X_LANES), lambda b, h: (offs + b, 0)),
        pl.BlockSpec((nc, V7X_SUBLANES, L), lambda b, h: (offs + b, h, 0)),
        pl.BlockSpec((nc, V7X_SUBLANES, L), lambda b, h: (offs + b, heads + h, 0)),
        pl.BlockSpec((None, None, 2, None, dk, dv), lambda b, h: (b, jl, 0, h, 0, 0)),
        pl.BlockSpec((None, None, 2, heads, dk), lambda b, h: (b, jl, 0, 0, 0)),
        pl.BlockSpec((None, None, 2, None, 1, 1), lambda b, h: (b, jl, 0, h, 0, 0)),
        pl.BlockSpec(memory_space=pl.ANY),
    ]
    return pl.pallas_call(
        functools.partial(_mlstm_sample_kernel, scale=dk ** -0.5, L=L, nc=nc, heads=heads),
        grid=(geo.bs, heads),
        in_specs=in_specs,
        out_specs=pl.BlockSpec((ts, dv), lambda b, h: (offs + b, h)),
        out_shape=jax.ShapeDtypeStruct(hbuf.shape, F32),
        scratch_shapes=[pltpu.VMEM((dk, dv), F32), pltpu.VMEM((1, dk), F32), pltpu.VMEM((1, 1), F32)],
        input_output_aliases={9: 0},
        compiler_params=_cparams(("arbitrary", "arbitrary")),
    )(z, z, z, gcol, grow, grow, state_c, state_n, state_m5, hbuf)


def _mlstm_out_kernel(h_ref, o_ref, g_ref, out_ref, *, heads, dv):
    for hd in range(heads):
        sl = slice(hd * dv, (hd + 1) * dv)
        y = _rms(h_ref[:, sl]) * g_ref[hd:hd + 1, :]
        out_ref[:, sl] = (y * jax.nn.sigmoid(o_ref[:, sl].astype(F32))).astype(BF16)


def mlstm_out(hsum, z, g_head3, jl, heads, dv, tm=256):
    n, d = hsum.shape
    ocol = (z.shape[1] - d) // d
    return pl.pallas_call(
        functools.partial(_mlstm_out_kernel, heads=heads, dv=dv),
        grid=(n // tm,),
        in_specs=[
            pl.BlockSpec((tm, d), lambda i: (i, 0)),
            pl.BlockSpec((tm, d), lambda i: (i, ocol)),
            pl.BlockSpec((None, heads, dv), lambda i: (jl, 0, 0)),
        ],
        out_specs=pl.BlockSpec((tm, d), lambda i: (i, 0)),
        out_shape=jax.ShapeDtypeStruct((n, d), BF16),
        compiler_params=_cparams(("arbitrary",)),
    )(hsum, z, g_head3)


def _rope_kernel(q_ref, k_ref, c_ref, s_ref, qo_ref, ko_ref):
    cos = c_ref[...]
    sin = s_ref[...]
    lane = lax.broadcasted_iota(I32, cos.shape, 1)
    low = (lane & 32) == 0
    for ref, oref in ((q_ref, qo_ref), (k_ref, ko_ref)):
        for g in range(ref.shape[1] // V7X_LANES):
            sl = slice(g * V7X_LANES, (g + 1) * V7X_LANES)
            x = ref[:, sl]
            partner = jnp.where(low, pltpu.roll(x, V7X_LANES - 32, 1), pltpu.roll(x, 32, 1))
            oref[:, sl] = (x * cos + partner * sin).astype(BF16)


def rope_tables(ts, dh):
    half = dh // 2
    t = jnp.arange(ts)
    row = (t // GRID_W).astype(F32)
    col = (t % GRID_W).astype(F32)
    inv = ROPE_THETA ** (-jnp.arange(0, half, 2, dtype=F32) / half)
    ar = row[:, None] * inv
    ac = col[:, None] * inv
    cos = jnp.concatenate([jnp.cos(ar), jnp.cos(ar), jnp.cos(ac), jnp.cos(ac)], axis=-1)
    sin = jnp.concatenate([-jnp.sin(ar), jnp.sin(ar), -jnp.sin(ac), jnp.sin(ac)], axis=-1)
    return cos, sin


def rope_sample(qkv, cos, sin, geo, d, tm):
    off = geo.np_ // tm
    per = geo.ts // tm
    return pl.pallas_call(
        _rope_kernel,
        grid=(geo.ns // tm,),
        in_specs=[
            pl.BlockSpec((tm, d), lambda i: (off + i, 0)),
            pl.BlockSpec((tm, d), lambda i: (off + i, 1)),
            pl.BlockSpec((tm, V7X_LANES), lambda i: (i % per, 0)),
            pl.BlockSpec((tm, V7X_LANES), lambda i: (i % per, 0)),
        ],
        out_specs=[pl.BlockSpec((tm, d), lambda i: (i, 0)),
                   pl.BlockSpec((tm, d), lambda i: (i, 0))],
        out_shape=[jax.ShapeDtypeStruct((geo.ns, d), BF16),
                   jax.ShapeDtypeStruct((geo.ns, d), BF16)],
        compiler_params=_cparams(("arbitrary",)),
    )(qkv, qkv, cos, sin)


def _attn_kernel(*refs, dh, lam_init, cached):
    if cached:
        (lam_ref, q_ref, k_ref, v_ref, kc_ref, vc_ref, g_ref, _, o_ref,
         kk_sc, vv_sc, s_sc, e_sc) = refs
        new_kv = pl.program_id(2) == 0
    else:
        lam_ref, q_ref, k_ref, v_ref, g_ref, o_ref, kk_sc, vv_sc, s_sc, e_sc = refs
        new_kv = True
    s_new = k_ref.shape[0]

    @pl.when(new_kv)
    def _():
        kk_sc[0:s_new, :] = k_ref[...].astype(BF16)
        vv_sc[0:s_new, :] = v_ref[...].astype(BF16)
        if cached:
            kk_sc[s_new:, :] = kc_ref[...].astype(BF16)
            vv_sc[s_new:, :] = vc_ref[...].astype(BF16)

    lp = lam_ref[...]
    lam = (jnp.exp(jnp.sum(lp[0:1] * lp[1:2], axis=1, keepdims=True))
           - jnp.exp(jnp.sum(lp[2:3] * lp[3:4], axis=1, keepdims=True)) + lam_init)
    c2 = dh ** -0.5 * math.log2(math.e)
    tq = q_ref.shape[0]
    ck = V7X_LANES
    n_ck = kk_sc.shape[0] // ck
    qs = [q_ref[:, c * dh:(c + 1) * dh].astype(BF16) for c in range(2)]

    def scores(c, j):
        s = lax.dot_general(qs[c], kk_sc[j * ck:(j + 1) * ck, c * dh:(c + 1) * dh], NT_DIMS,
                            preferred_element_type=F32)
        s_sc[c, :, j * ck:(j + 1) * ck] = s
        return s

    def probs(c, j, m):
        e = jnp.exp2((s_sc[c, :, j * ck:(j + 1) * ck] - m) * c2)
        e_sc[c, :, j * ck:(j + 1) * ck] = e.astype(BF16)
        return e

    def pv(c, lv):
        l = jnp.sum(lv, axis=1, keepdims=True)
        return jnp.dot(e_sc[c], vv_sc[...], preferred_element_type=F32) * (1.0 / l)

    ninf = jnp.full((tq, ck), -jnp.inf, F32)
    zero = jnp.zeros((tq, ck), F32)
    mv0, mv1, lv0, lv1 = ninf, ninf, zero, zero
    for j in range(n_ck):
        mv0 = jnp.maximum(mv0, scores(0, j))
    m0 = jnp.max(mv0, axis=1, keepdims=True)
    for j in range(n_ck):
        lv0 = lv0 + probs(0, j, m0)
        mv1 = jnp.maximum(mv1, scores(1, j))
    m1 = jnp.max(mv1, axis=1, keepdims=True)
    o0 = pv(0, lv0)
    for j in range(n_ck):
        lv1 = lv1 + probs(1, j, m1)
    o = o0 - lam * pv(1, lv1)
    o_ref[...] = (_rms(o) * g_ref[...] * (1.0 - lam_init)).astype(BF16)


def _attn_scratch(tq, s_total, w):
    return [pltpu.VMEM((s_total, w), BF16), pltpu.VMEM((s_total, w), BF16),
            pltpu.VMEM((2, tq, s_total), F32), pltpu.VMEM((2, tq, s_total), BF16)]


def attn_prompt(qkv, a_lam, g_sub4, jl, geo, heads, dh, lam_init, out_rows):
    t = geo.tp
    w = 2 * dh
    return pl.pallas_call(
        functools.partial(_attn_kernel, dh=dh, lam_init=lam_init, cached=False),
        grid=(geo.bp, heads),
        in_specs=[
            pl.BlockSpec((None, 4, dh), lambda b, h: (jl, 0, 0)),
            pl.BlockSpec((t, w), lambda b, h: (b, h)),
            pl.BlockSpec((t, w), lambda b, h: (b, heads + h)),
            pl.BlockSpec((t, w), lambda b, h: (b, 2 * heads + h)),
            pl.BlockSpec((None, None, 1, w), lambda b, h: (jl, h, 0, 0)),
        ],
        out_specs=pl.BlockSpec((t, w), lambda b, h: (b, h)),
        out_shape=jax.ShapeDtypeStruct((out_rows, heads * w), BF16),
        scratch_shapes=_attn_scratch(t, t, w),
        compiler_params=_cparams(("arbitrary", "arbitrary")),
    )(a_lam, qkv, qkv, qkv, g_sub4)


def attn_sample(q_rot, k_rot, qkv, cache_k4, cache_v4, a_lam, g_sub4, jl, geo, heads, dh,
                lam_init, prev, tq):
    w = 2 * dh
    nq = geo.ts // tq
    offq = geo.np_ // tq
    offs = geo.np_ // geo.ts
    past = cache_k4.shape[2]
    return pl.pallas_call(
        functools.partial(_attn_kernel, dh=dh, lam_init=lam_init, cached=True),
        grid=(geo.bs, heads, nq),
        in_specs=[
            pl.BlockSpec((None, 4, dh), lambda b, h, i: (jl, 0, 0)),
            pl.BlockSpec((tq, w), lambda b, h, i: (b * nq + i, h)),
            pl.BlockSpec((geo.ts, w), lambda b, h, i: (b, h)),
            pl.BlockSpec((geo.ts, w), lambda b, h, i: (offs + b, 2 * heads + h)),
            pl.BlockSpec((None, None, past, w), lambda b, h, i: (b, jl, 0, h)),
            pl.BlockSpec((None, None, past, w), lambda b, h, i: (b, jl, 0, h)),
            pl.BlockSpec((None, None, 1, w), lambda b, h, i: (jl, h, 0, 0)),
            pl.BlockSpec(memory_space=pl.ANY),
        ],
        out_specs=pl.BlockSpec((tq, w), lambda b, h, i: (offq + b * nq + i, h)),
        out_shape=jax.ShapeDtypeStruct(prev.shape, BF16),
        input_output_aliases={7: 0},
        scratch_shapes=_attn_scratch(tq, geo.ts + past, w),
        compiler_params=_cparams(("arbitrary", "arbitrary", "arbitrary")),
    )(a_lam, q_rot, k_rot, qkv, cache_k4, cache_v4, g_sub4, prev)


def _router_kernel(x_ref, g_ref, sh_ref, sc_ref, rwh_ref, rwl_ref, rb_ref,
                   h_ref, ei_ref, rk_ref, gwc_ref, cnt_ref, carry_sc, *, n_exp):
    i = pl.program_id(0)

    @pl.when(i == 0)
    def _():
        carry_sc[...] = jnp.zeros_like(carry_sc)

    y = _rms(x_ref[...]) * g_ref[...]
    hm = y * (1.0 + sc_ref[...]) + sh_ref[...]
    h_ref[...] = hm
    tm = hm.shape[0]
    hi = hm.astype(BF16)
    lo = (hm - hi.astype(F32)).astype(BF16)
    rwh = rwh_ref[...]
    logits = (lax.dot_general(rwh, hi, NT_DIMS, preferred_element_type=F32)
              + lax.dot_general(rwh, lo, NT_DIMS, preferred_element_type=F32)
              + lax.dot_general(rwl_ref[...], hi, NT_DIMS, preferred_element_type=F32))
    ex = jnp.exp(logits - jnp.max(logits, axis=0, keepdims=True))
    probs = ex / jnp.sum(ex, axis=0, keepdims=True)
    sel = probs + rb_ref[...]
    epg = n_exp // N_GROUPS
    sub = lax.broadcasted_iota(I32, (epg, tm), 0).astype(F32)
    ninf = -jnp.inf

    def top2(v):
        m1 = jnp.max(v, axis=0, keepdims=True)
        a1 = jnp.min(jnp.where(v == m1, sub, float(epg)), axis=0, keepdims=True)
        v2 = jnp.where(sub == a1, ninf, v)
        m2 = jnp.max(v2, axis=0, keepdims=True)
        a2 = jnp.min(jnp.where(v2 == m2, sub, float(epg)), axis=0, keepdims=True)
        return m1, a1, m2, a2

    best = None
    gi = None
    for g in range(N_GROUPS):
        m1, _, m2, _ = top2(sel[g * epg:(g + 1) * epg, :])
        score = m1 + m2
        if g == 0:
            best, gi = score, jnp.zeros((1, tm), F32)
        else:
            upd = score > best
            best = jnp.where(upd, score, best)
            gi = jnp.where(upd, float(g), gi)
    ing = jnp.zeros((epg, tm), F32)
    pin = jnp.zeros((epg, tm), F32)
    for g in range(N_GROUPS):
        pick = gi == float(g)
        ing = jnp.where(pick, sel[g * epg:(g + 1) * epg, :], ing)
        pin = jnp.where(pick, probs[g * epg:(g + 1) * epg, :], pin)
    _, l1, _, l2 = top2(ing)
    w1 = jnp.sum(jnp.where(sub == l1, pin, 0.0), axis=0, keepdims=True)
    w2 = jnp.sum(jnp.where(sub == l2, pin, 0.0), axis=0, keepdims=True)
    ws = w1 + w2
    e1 = (gi * epg + l1).astype(I32)
    e2 = (gi * epg + l2).astype(I32)

    eio = lax.broadcasted_iota(I32, (n_exp, tm), 0)
    is1 = eio == e1
    is2 = eio == e2
    mem = jnp.where(is1 | is2, 1.0, 0.0)
    tri = jnp.where(lax.broadcasted_iota(I32, (tm, tm), 0) < lax.broadcasted_iota(I32, (tm, tm), 1),
                    1.0, 0.0).astype(BF16)
    pre = jnp.dot(mem.astype(BF16), tri, preferred_element_type=F32) + carry_sc[...]
    r1 = jnp.sum(jnp.where(is1, pre, 0.0), axis=0, keepdims=True)
    r2 = jnp.sum(jnp.where(is2, pre, 0.0), axis=0, keepdims=True)
    carry = carry_sc[...] + jnp.sum(mem, axis=1, keepdims=True)
    carry_sc[...] = carry
    cnt_ref[...] = jnp.broadcast_to(carry, cnt_ref.shape)

    row8 = lax.broadcasted_iota(I32, (V7X_SUBLANES, tm), 0)
    ei_ref[...] = jnp.where(row8 == 0, e1, jnp.where(row8 == 1, e2, 0))
    rk_ref[...] = jnp.where(row8 == 0, r1.astype(I32), jnp.where(row8 == 1, r2.astype(I32), 0))
    row128 = lax.broadcasted_iota(I32, (V7X_LANES, tm), 0)
    gw = jnp.where(row128 == 0, w1 / ws, jnp.where(row128 == 1, w2 / ws, 0.0))
    gwc_ref[...] = gw.T


def moe_router(x, g_norm4, mod, rwt_hi, rwt_lo, rb_col, geo, layer, tm=256):
    n, d = x.shape
    n_exp = rwt_hi.shape[0]
    return pl.pallas_call(
        functools.partial(_router_kernel, n_exp=n_exp),
        grid=(n // tm,),
        in_specs=[
            pl.BlockSpec((tm, d), lambda i: (i, 0)),
            pl.BlockSpec((None, None, 1, d), lambda i: (layer, 1, 0, 0)),
            pl.BlockSpec((None, None, None, 1, d), lambda i: (layer, geo.mod_row(i, tm), 3, 0, 0)),
            pl.BlockSpec((None, None, None, 1, d), lambda i: (layer, geo.mod_row(i, tm), 4, 0, 0)),
            pl.BlockSpec((n_exp, d), lambda i: (0, 0)),
            pl.BlockSpec((n_exp, d), lambda i: (0, 0)),
            pl.BlockSpec((n_exp, 1), lambda i: (0, 0)),
        ],
        out_specs=[
            pl.BlockSpec((tm, d), lambda i: (i, 0)),
            pl.BlockSpec((V7X_SUBLANES, tm), lambda i: (0, i)),
            pl.BlockSpec((V7X_SUBLANES, tm), lambda i: (0, i)),
            pl.BlockSpec((tm, V7X_LANES), lambda i: (i, 0)),
            pl.BlockSpec((n_exp, V7X_LANES), lambda i: (0, 0)),
        ],
        out_shape=[
            jax.ShapeDtypeStruct((n, d), F32),
            jax.ShapeDtypeStruct((V7X_SUBLANES, n), I32),
            jax.ShapeDtypeStruct((V7X_SUBLANES, n), I32),
            jax.ShapeDtypeStruct((n, V7X_LANES), F32),
            jax.ShapeDtypeStruct((n_exp, V7X_LANES), F32),
        ],
        scratch_shapes=[pltpu.VMEM((n_exp, 1), F32)],
        compiler_params=_cparams(("arbitrary",)),
    )(x, g_norm4, mod, mod, rwt_hi, rwt_lo, rb_col)


def _gather_kernel(tok_ref, nused_ref, h_hbm, o_ref, buf, sem, *, bm):
    blk = pl.program_id(0)
    n_used = nused_ref[0]

    def row_copy(b, r):
        slot = b % 2
        return pltpu.make_async_copy(h_hbm.at[pl.ds(tok_ref[b * bm + r], 1)],
                                     buf.at[slot, pl.ds(r, 1)], sem.at[slot])

    def start_block(b):
        def start(r, carry):
            row_copy(b, r).start()
            return carry
        lax.fori_loop(0, bm, start, 0, unroll=8)

    @pl.when(blk == 0)
    def _():
        start_block(blk)

    @pl.when(blk + 1 < n_used)
    def _():
        start_block(blk + 1)

    @pl.when(blk < n_used)
    def _():
        def wait(r, carry):
            row_copy(blk, r).wait()
            return carry
        lax.fori_loop(0, bm, wait, 0, unroll=8)
        o_ref[...] = buf[blk % 2].astype(BF16)


def moe_gather(h2, slot_tok, n_used, n_blocks, bm):
    n, d = h2.shape
    return pl.pallas_call(
        functools.partial(_gather_kernel, bm=bm),
        grid_spec=pltpu.PrefetchScalarGridSpec(
            num_scalar_prefetch=2,
            grid=(n_blocks,),
            in_specs=[pl.BlockSpec(memory_space=pl.ANY)],
            out_specs=pl.BlockSpec((bm, d), lambda b, tok, nu: (b, 0)),
            scratch_shapes=[pltpu.VMEM((2, bm, d), F32), pltpu.SemaphoreType.DMA((2,))],
        ),
        out_shape=jax.ShapeDtypeStruct((n_blocks * bm, d), BF16),
        compiler_params=_cparams(("arbitrary",)),
    )(slot_tok, n_used, h2)


def _moe_up_kernel(blk_ref, f_ref, e_ref, first_ref, valid_ref,
                   x_ref, wg_ref, wu_ref, a_ref, wg_sc, wu_sc):
    w = pl.program_id(0)

    @pl.when(valid_ref[w] == 1)
    def _():
        @pl.when(first_ref[w] == 1)
        def _():
            wg_sc[...] = wg_ref[...].astype(BF16)
            wu_sc[...] = wu_ref[...].astype(BF16)

        x = x_ref[...]
        g = jnp.dot(x, wg_sc[...], preferred_element_type=F32)
        u = jnp.dot(x, wu_sc[...], preferred_element_type=F32)
        a_ref[...] = (_silu(g) * u).astype(BF16)


def moe_up(xs, tabs, w_gate, w_up, layer, bm, tf):
    n_slots, d = xs.shape
    ff = w_gate.shape[3]
    nw = tabs[0].shape[0]
    wspec = pl.BlockSpec((None, None, d, tf), lambda w, blk, f, e, fi, va: (layer, e[w], 0, f[w]))
    return pl.pallas_call(
        _moe_up_kernel,
        grid_spec=pltpu.PrefetchScalarGridSpec(
            num_scalar_prefetch=5,
            grid=(nw,),
            in_specs=[pl.BlockSpec((bm, d), lambda w, blk, f, e, fi, va: (blk[w], 0)), wspec, wspec],
            out_specs=pl.BlockSpec((bm, tf), lambda w, blk, f, e, fi, va: (blk[w], f[w])),
            scratch_shapes=[pltpu.VMEM((d, tf), BF16), pltpu.VMEM((d, tf), BF16)],
        ),
        out_shape=jax.ShapeDtypeStruct((n_slots, ff), BF16),
        compiler_params=_cparams(("arbitrary",)),
    )(*tabs, xs, w_gate, w_up)


def _moe_down_kernel(blk_ref, f_ref, e_ref, first_ref, valid_ref, a_ref, wd_ref, y_ref, wd_sc):
    w = pl.program_id(0)

    @pl.when(valid_ref[w] == 1)
    def _():
        @pl.when(first_ref[w] == 1)
        def _():
            wd_sc[...] = wd_ref[...].astype(BF16)

        y_ref[...] = jnp.dot(a_ref[...], wd_sc[...], preferred_element_type=F32)


def moe_down(act, tabs, w_down, layer, bm, tn):
    n_slots, ff = act.shape
    d = w_down.shape[3]
    nw = tabs[0].shape[0]
    return pl.pallas_call(
        _moe_down_kernel,
        grid_spec=pltpu.PrefetchScalarGridSpec(
            num_scalar_prefetch=5,
            grid=(nw,),
            in_specs=[
                pl.BlockSpec((bm, ff), lambda w, blk, f, e, fi, va: (blk[w], 0)),
                pl.BlockSpec((None, None, ff, tn), lambda w, blk, f, e, fi, va: (layer, e[w], 0, f[w])),
            ],
            out_specs=pl.BlockSpec((bm, tn), lambda w, blk, f, e, fi, va: (blk[w], f[w])),
            scratch_shapes=[pltpu.VMEM((ff, tn), BF16)],
        ),
        out_shape=jax.ShapeDtypeStruct((n_slots, d), F32),
        compiler_params=_cparams(("arbitrary",)),
    )(*tabs, act, w_down)


def _work_tables(nb_e, n_inner, n_blocks):
    n_exp = nb_e.shape[0]
    s_e = jnp.cumsum(nb_e) - nb_e
    tot = jnp.sum(nb_e)
    wstart = n_inner * s_e
    wend = wstart + n_inner * nb_e
    nw = n_inner * n_blocks
    n_valid = n_inner * tot
    w = jnp.minimum(jnp.arange(nw, dtype=I32), n_valid - 1)
    e = jnp.minimum(jnp.sum((wend[None, :] <= w[:, None]).astype(I32), axis=1), n_exp - 1)
    r = w - wstart[e]
    nb = jnp.maximum(nb_e[e], 1)
    f = r // nb
    p = r % nb
    blk = s_e[e] + p
    valid = (jnp.arange(nw, dtype=I32) < n_valid).astype(I32)
    first = valid * (p == 0).astype(I32)
    return (blk.astype(I32), f.astype(I32), e, first, valid)


def _combine_kernel(dest_ref, x_ref, g_ref, gwc_ref, y_hbm, o_ref, buf, sem, *, tb, n_tok):
    i = pl.program_id(0)

    def row_copy(blk, k, r):
        slot = blk % 2
        return pltpu.make_async_copy(y_hbm.at[pl.ds(dest_ref[k * n_tok + blk * tb + r], 1)],
                                     buf.at[slot, k, pl.ds(r, 1)], sem.at[slot])

    def start_block(blk):
        def start(r, carry):
            row_copy(blk, 0, r).start()
            row_copy(blk, 1, r).start()
            return carry
        lax.fori_loop(0, tb, start, 0, unroll=8)

    @pl.when(i == 0)
    def _():
        start_block(i)

    @pl.when(i + 1 < pl.num_programs(0))
    def _():
        start_block(i + 1)

    def wait(r, carry):
        row_copy(i, 0, r).wait()
        row_copy(i, 1, r).wait()
        return carry

    lax.fori_loop(0, tb, wait, 0, unroll=8)
    slot = i % 2
    gwc = gwc_ref[...]
    y = gwc[:, 0:1] * buf[slot, 0] + gwc[:, 1:2] * buf[slot, 1]
    o_ref[...] = x_ref[...] + g_ref[...] * y


def moe_combine(x, yb, dest, gwc, mod, geo, layer, tb=256):
    n, d = x.shape
    return pl.pallas_call(
        functools.partial(_combine_kernel, tb=tb, n_tok=n),
        grid_spec=pltpu.PrefetchScalarGridSpec(
            num_scalar_prefetch=1,
            grid=(n // tb,),
            in_specs=[
                pl.BlockSpec((tb, d), lambda i, dr: (i, 0)),
                pl.BlockSpec((None, None, None, 1, d), lambda i, dr: (layer, geo.mod_row(i, tb), 5, 0, 0)),
                pl.BlockSpec((tb, V7X_LANES), lambda i, dr: (i, 0)),
                pl.BlockSpec(memory_space=pl.ANY),
            ],
            out_specs=pl.BlockSpec((tb, d), lambda i, dr: (i, 0)),
            scratch_shapes=[pltpu.VMEM((2, 2, tb, d), F32), pltpu.SemaphoreType.DMA((2,))],
        ),
        out_shape=jax.ShapeDtypeStruct((n, d), F32),
        compiler_params=_cparams(("arbitrary",)),
    )(dest, x, mod, gwc, yb)


def moe_layer(x, g_norm4, mod, rwt_hi, rwt_lo, rb_col, w_gate, w_up, w_down, geo, layer):
    n, d = x.shape
    n_exp = rwt_hi.shape[0]
    bm = MOE_BM
    h2, ei, rk, gwc, cnt = moe_router(x, g_norm4, mod, rwt_hi, rwt_lo, rb_col, geo, layer)
    counts = cnt[:, 0].astype(I32)
    nb_e = (counts + bm - 1) // bm
    pad_start = (jnp.cumsum(nb_e) - nb_e) * bm
    e_flat = ei[:2].reshape(-1)
    dest = pad_start[e_flat] + rk[:2].reshape(-1)
    n_blocks = (2 * n) // bm + n_exp
    tok = jnp.tile(jnp.arange(n, dtype=I32), 2)
    slot_tok = jnp.zeros((n_blocks * bm,), I32).at[dest].set(tok)
    n_used = jnp.sum(nb_e).astype(I32).reshape(1)
    xs = moe_gather(h2, slot_tok, n_used, n_blocks, bm)
    ff = w_gate.shape[3]
    tf = min(MOE_TF, ff)
    act = moe_up(xs, _work_tables(nb_e, ff // tf, n_blocks), w_gate, w_up, layer, bm, tf)
    yb = moe_down(act, _work_tables(nb_e, 1, n_blocks), w_down, layer, bm, d)
    return moe_combine(x, yb, dest, gwc, mod, geo, layer)


def _final_kernel(x_ref, g_ref, o_ref):
    o_ref[...] = _rms(x_ref[...]) * g_ref[...]


def final_norm(x, g_row, row0, rows, tm=256):
    d = x.shape[1]
    off = row0 // tm
    return pl.pallas_call(
        _final_kernel,
        grid=(rows // tm,),
        in_specs=[pl.BlockSpec((tm, d), lambda i: (off + i, 0)),
                  pl.BlockSpec((1, d), lambda i: (0, 0))],
        out_specs=pl.BlockSpec((tm, d), lambda i: (i, 0)),
        out_shape=jax.ShapeDtypeStruct((rows, d), F32),
        compiler_params=_cparams(("arbitrary",)),
    )(x, g_row)


def _gate_weights(w_in, b_gate, n_main, heads):
    j = jnp.arange(V7X_LANES) // V7X_SUBLANES
    d_, h_ = j // heads, j % heads
    ci = n_main + d_ * 2 * heads + h_
    cf = ci + heads
    wi = w_in[:, ci].astype(BF16)
    wf = w_in[:, cf].astype(BF16)
    bi = b_gate[ci - n_main].reshape(1, -1)
    bf = b_gate[cf - n_main].reshape(1, -1)
    return wi, wf, bi, bf


def kernel(x_prompt, x_sample, state_C, state_n, state_m, cache_k, cache_v, c, c_ctx,
           w_mod, b_mod, g_norm, m_w_in, m_b_gate, m_g_head, m_w_out,
           a_w_qkv, a_lam, a_g_sub, a_w_o, router_w, router_b, w_gate, w_up, w_down, g_final):
    bp, tp, d = x_prompt.shape
    bs, ts, _ = x_sample.shape
    depth = w_mod.shape[0]
    geo = Geo(bp, tp, bs, ts)
    tm = geo.tm
    m_heads, m_dv = m_g_head.shape[1], m_g_head.shape[2]
    m_dk = m_dv // 2
    a_heads = a_g_sub.shape[1]
    a_dh = a_g_sub.shape[2] // 2
    n_exp = router_w.shape[1]
    assert tp == MLSTM_CHUNK and ts % MLSTM_CHUNK == 0 and geo.np_ % ts == 0
    assert 2 * m_heads * V7X_SUBLANES == V7X_LANES and n_exp // N_GROUPS == V7X_SUBLANES

    x = jnp.concatenate([x_prompt.reshape(geo.np_, d), x_sample.reshape(geo.ns, d)], axis=0)

    n_rows = 2 * V7X_SUBLANES
    cv = jnp.zeros((n_rows, d), F32).at[0].set(c_ctx).at[1:1 + bs].set(c)
    mod = adaln_all(cv, w_mod, b_mod).reshape(depth, n_rows, 6, 1, d)
    g_norm4 = g_norm.reshape(depth, 2, 1, d)

    rwt = router_w.T
    rwt_hi = rwt.astype(BF16)
    rwt_lo = (rwt - rwt_hi.astype(F32)).astype(BF16)
    rb_col = router_b.reshape(n_exp, 1)

    new_c, new_n, new_m, new_k, new_v = [], [], [], [], []
    for i in range(depth):
        jl = i // N_MIXERS
        h = modnorm(x, g_norm4, mod, geo, i, 0, 0, 1)
        if i % N_MIXERS == 0:
            n_main = 2 * m_heads * m_dk + 2 * m_heads * m_dv
            z = matmul(h, m_w_in, jl, n_main, tm, out_dtype=BF16)
            wi, wf, bi, bf = _gate_weights(m_w_in[jl], m_b_gate[jl], n_main, m_heads)
            L = MLSTM_CHUNK
            gcol, grow = mlstm_gates(h, wi, wf, bi, bf, L, tm)
            hsum, c_new, n_new, m_new = mlstm_prompt(z, gcol, grow, geo, m_heads, m_dk, m_dv)
            new_c.append(c_new)
            new_n.append(n_new)
            new_m.append(m_new.reshape(bp, 2, m_heads))
            state_m5 = state_m.reshape(state_m.shape + (1, 1))
            hsum = mlstm_sample(z, gcol, grow, state_C, state_n, state_m5, jl, geo, m_heads,
                                m_dk, m_dv, L, hsum)
            mix_in = mlstm_out(hsum, z, m_g_head, jl, m_heads, m_dv)
            x = matmul_residual(mix_in, m_w_out, jl, x, mod, geo, i, 2, tm)
        else:
            lam_init = 0.8 - 0.6 * math.exp(-0.3 * i)
            qkv = matmul(h, a_w_qkv, jl, 3 * d, tm)
            new_k.append(qkv[:geo.np_, d:2 * d].reshape(bp, tp, a_heads, 2, a_dh))
            new_v.append(qkv[:geo.np_, 2 * d:].reshape(bp, tp, a_heads, 2 * a_dh))
            g_sub4 = a_g_sub.reshape(a_g_sub.shape[0], a_heads, 1, 2 * a_dh)
            cos, sin = rope_tables(ts, a_dh)
            q_rot, k_rot = rope_sample(qkv, cos, sin, geo, d, min(512, ts))
            att = attn_prompt(qkv, a_lam, g_sub4, jl, geo, a_heads, a_dh, lam_init, geo.n)
            ck4 = cache_k.reshape(bs, cache_k.shape[1], cache_k.shape[2], d)
            cv4 = cache_v.reshape(bs, cache_v.shape[1], cache_v.shape[2], d)
            att = attn_sample(q_rot, k_rot, qkv, ck4, cv4, a_lam, g_sub4, jl, geo, a_heads, a_dh,
                              lam_init, att, min(ATTN_TQ, ts))
            x = matmul_residual(att, a_w_o, jl, x, mod, geo, i, 2, tm)
        x = moe_layer(x, g_norm4, mod, rwt_hi, rwt_lo, rb_col, w_gate, w_up, w_down, geo, i)

    g_row = g_final.reshape(1, d)
    y_prompt = final_norm(x, g_row, 0, geo.np_).reshape(bp, tp, d)
    y_sample = final_norm(x, g_row, geo.np_, geo.ns).reshape(bs, ts, d)
    return (y_prompt, y_sample,
            jnp.stack(new_c, axis=1), jnp.stack(new_n, axis=1), jnp.stack(new_m, axis=1),
            jnp.stack(new_k, axis=1), jnp.stack(new_v, axis=1))
```

```python
import functools
import math

import jax
import jax.numpy as jnp
from jax import lax
from jax.experimental import pallas as pl
from jax.experimental.pallas import tpu as pltpu

F32 = jnp.float32
BF16 = jnp.bfloat16
I32 = jnp.int32

EPS = 1e-6
N_GROUPS = 8
GRID_W = 64
ROPE_THETA = 10000.0
N_MIXERS = 2

V7X_LANES = 128
V7X_SUBLANES = 8
V7X_VMEM_LIMIT = 56 * 1024 * 1024

MLSTM_CHUNK = 256
MOE_BM = 256
MOE_TF = 512
ATTN_TQ = 256

NT_DIMS = (((1,), (1,)), ((), ()))
TN_DIMS = (((0,), (0,)), ((), ()))


def _cparams(sem):
    return pltpu.CompilerParams(dimension_semantics=sem, vmem_limit_bytes=V7X_VMEM_LIMIT)


def _silu(x):
    return x * jax.nn.sigmoid(x)


def _rms(x):
    return x * lax.rsqrt(jnp.mean(x * x, axis=-1, keepdims=True) + EPS)


def _adaln_kernel(cv_ref, w_ref, b_ref, o_ref):
    s = _silu(cv_ref[...]).astype(BF16)
    o_ref[...] = jnp.dot(s, w_ref[...].astype(BF16), preferred_element_type=F32) + b_ref[...]


def adaln_all(cv, w_mod, b_mod, tn=512):
    depth, d, n6 = w_mod.shape
    r = cv.shape[0]
    return pl.pallas_call(
        _adaln_kernel,
        grid=(depth, n6 // tn),
        in_specs=[
            pl.BlockSpec((r, d), lambda l, j: (0, 0)),
            pl.BlockSpec((None, d, tn), lambda l, j: (l, 0, j)),
            pl.BlockSpec((None, 1, tn), lambda l, j: (l, 0, j)),
        ],
        out_specs=pl.BlockSpec((None, r, tn), lambda l, j: (l, 0, j)),
        out_shape=jax.ShapeDtypeStruct((depth, r, n6), F32),
        compiler_params=_cparams(("arbitrary", "arbitrary")),
    )(cv, w_mod, b_mod.reshape(depth, 1, n6))


class Geo:
    def __init__(self, bp, tp, bs, ts):
        self.bp, self.tp, self.bs, self.ts = bp, tp, bs, ts
        self.np_ = bp * tp
        self.ns = bs * ts
        self.n = self.np_ + self.ns
        self.tm = min(1024, math.gcd(self.np_, ts))

    def mod_row(self, i, tm):
        r0 = i * tm
        return jnp.where(r0 < self.np_, 0, 1 + (r0 - self.np_) // self.ts)


def _modnorm_kernel(x_ref, g_ref, sh_ref, sc_ref, o_ref):
    y = _rms(x_ref[...]) * g_ref[...]
    o_ref[...] = (y * (1.0 + sc_ref[...]) + sh_ref[...]).astype(o_ref.dtype)


def modnorm(x, g_norm4, mod, geo, layer, which_g, which_sh, which_sc, tm=256):
    n, d = x.shape
    return pl.pallas_call(
        _modnorm_kernel,
        grid=(n // tm,),
        in_specs=[
            pl.BlockSpec((tm, d), lambda i: (i, 0)),
            pl.BlockSpec((None, None, 1, d), lambda i: (layer, which_g, 0, 0)),
            pl.BlockSpec((None, None, None, 1, d), lambda i: (layer, geo.mod_row(i, tm), which_sh, 0, 0)),
            pl.BlockSpec((None, None, None, 1, d), lambda i: (layer, geo.mod_row(i, tm), which_sc, 0, 0)),
        ],
        out_specs=pl.BlockSpec((tm, d), lambda i: (i, 0)),
        out_shape=jax.ShapeDtypeStruct((n, d), BF16),
        compiler_params=_cparams(("arbitrary",)),
    )(x, g_norm4, mod, mod)


def _mm_kernel(a_ref, w_ref, o_ref):
    o_ref[...] = jnp.dot(a_ref[...], w_ref[...].astype(BF16),
                         preferred_element_type=F32).astype(o_ref.dtype)


def _mm_res_kernel(a_ref, w_ref, x_ref, g_ref, o_ref):
    acc = jnp.dot(a_ref[...], w_ref[...].astype(BF16), preferred_element_type=F32)
    o_ref[...] = x_ref[...] + g_ref[...] * acc


def matmul(a, w3, widx, n_cols, tm, tn=512, out_dtype=F32):
    m, k = a.shape
    return pl.pallas_call(
        _mm_kernel,
        grid=(m // tm, n_cols // tn),
        in_specs=[
            pl.BlockSpec((tm, k), lambda i, j: (i, 0)),
            pl.BlockSpec((None, k, tn), lambda i, j: (widx, 0, j)),
        ],
        out_specs=pl.BlockSpec((tm, tn), lambda i, j: (i, j)),
        out_shape=jax.ShapeDtypeStruct((m, n_cols), out_dtype),
        compiler_params=_cparams(("arbitrary", "arbitrary")),
    )(a, w3)


def matmul_residual(a, w3, widx, x, mod, geo, layer, which_gate, tm, tn=512):
    m, k = a.shape
    d = w3.shape[2]
    return pl.pallas_call(
        _mm_res_kernel,
        grid=(m // tm, d // tn),
        in_specs=[
            pl.BlockSpec((tm, k), lambda i, j: (i, 0)),
            pl.BlockSpec((None, k, tn), lambda i, j: (widx, 0, j)),
            pl.BlockSpec((tm, tn), lambda i, j: (i, j)),
            pl.BlockSpec((None, None, None, 1, tn),
                         lambda i, j: (layer, geo.mod_row(i, tm), which_gate, 0, j)),
        ],
        out_specs=pl.BlockSpec((tm, tn), lambda i, j: (i, j)),
        out_shape=jax.ShapeDtypeStruct((m, d), F32),
        compiler_params=_cparams(("arbitrary", "arbitrary")),
    )(a, w3, x, mod)


def _gates_kernel(h_ref, wi_ref, wf_ref, bi_ref, bf_ref, col_ref, row_ref, *, chunk):
    h = h_ref[...]
    tm = h.shape[0]
    gi = jnp.dot(h, wi_ref[...], preferred_element_type=F32) + bi_ref[...]
    xf = jnp.dot(h, wf_ref[...], preferred_element_type=F32) + bf_ref[...]
    lf = jnp.minimum(xf, 0.0) - jnp.log1p(jnp.exp(-jnp.abs(xf)))
    r = lax.broadcasted_iota(I32, (tm, V7X_LANES), 0) % chunk
    lane = lax.broadcasted_iota(I32, (tm, V7X_LANES), 1)
    pre = lf
    suf = lf
    s = 1
    while s < chunk:
        pre = pre + jnp.where(r >= s, pltpu.roll(pre, s, 0), 0.0)
        suf = suf + jnp.where(r < chunk - s, pltpu.roll(suf, tm - s, 0), 0.0)
        s *= 2
    b = jnp.where(lane < V7X_LANES // 2, pre, suf)
    sub = lane % V7X_SUBLANES
    col = jnp.where(sub == 0, gi - b, jnp.where(sub == 1, b, gi))
    col_ref[...] = col
    for ci in range(tm // chunk):
        row_ref[ci] = col[ci * chunk:(ci + 1) * chunk, :].T


def mlstm_gates(h, wi, wf, bi, bf, chunk, tm):
    n, d = h.shape
    return pl.pallas_call(
        functools.partial(_gates_kernel, chunk=chunk),
        grid=(n // tm,),
        in_specs=[
            pl.BlockSpec((tm, d), lambda i: (i, 0)),
            pl.BlockSpec((d, V7X_LANES), lambda i: (0, 0)),
            pl.BlockSpec((d, V7X_LANES), lambda i: (0, 0)),
            pl.BlockSpec((1, V7X_LANES), lambda i: (0, 0)),
            pl.BlockSpec((1, V7X_LANES), lambda i: (0, 0)),
        ],
        out_specs=[
            pl.BlockSpec((tm, V7X_LANES), lambda i: (i, 0)),
            pl.BlockSpec((tm // chunk, V7X_LANES, chunk), lambda i: (i, 0, 0)),
        ],
        out_shape=[jax.ShapeDtypeStruct((n, V7X_LANES), F32),
                   jax.ShapeDtypeStruct((n // chunk, V7X_LANES, chunk), F32)],
        compiler_params=_cparams(("arbitrary",)),
    )(h, wi, wf, bi, bf)


def _lane_pick(gc, lane_idx):
    lane = lax.broadcasted_iota(I32, gc.shape, 1)
    return jnp.sum(jnp.where(lane == lane_idx, gc, 0.0), axis=1, keepdims=True)


def _mlstm_chunk(qf, kf, vb, b_col, ig_col, a_row, state, reverse):
    L = qf.shape[0]
    qb = qf.astype(BF16)
    kb = kf.astype(BF16)
    ti = lax.broadcasted_iota(I32, (L, L), 0)
    si = lax.broadcasted_iota(I32, (L, L), 1)
    mask = (si >= ti) if reverse else (si <= ti)
    log_d = jnp.where(mask, b_col + a_row, -jnp.inf)
    mx = jnp.max(log_d, axis=1, keepdims=True)
    if state is None:
        m_old = jnp.zeros((1, 1), F32)
    else:
        c_old, n_old, m_old = state
    log_inter = b_col + m_old
    m_t = jnp.maximum(log_inter, mx)
    s = lax.dot_general(qb, kb, NT_DIMS, preferred_element_type=F32) * jnp.exp(log_d - m_t)
    num = jnp.dot(s.astype(BF16), vb, preferred_element_type=F32)
    den = jnp.sum(s, axis=1, keepdims=True)
    if state is not None:
        w_inter = jnp.exp(log_inter - m_t)
        num = num + w_inter * jnp.dot(qb, c_old.astype(BF16), preferred_element_type=F32)
        den = den + w_inter * jnp.sum(qf * n_old, axis=1, keepdims=True)
    h = num / jnp.maximum(jnp.abs(den), jnp.exp(-m_t))
    b_last = b_col[0:1, :] if reverse else b_col[L - 1:L, :]
    log_w = b_last - b_col + ig_col
    m_new = jnp.maximum(b_last + m_old, jnp.max(log_w, axis=0, keepdims=True))
    w_s = jnp.exp(log_w - m_new)
    kw = kf * w_s
    c_new = lax.dot_general(kw.astype(BF16), vb, TN_DIMS, preferred_element_type=F32)
    n_new = jnp.sum(kw, axis=0, keepdims=True)
    if state is not None:
        decay = jnp.exp(b_last + m_old - m_new)
        c_new = c_new + decay * c_old
        n_new = n_new + decay * n_old
    return h, (c_new, n_new, m_new)


def _mlstm_prompt_kernel(q_ref, k_ref, v_ref, gc_ref, rowf_ref, rowb_ref,
                         h_ref, c_ref, n_ref, m_ref, *, scale):
    hd = pl.program_id(1)
    qf = q_ref[...].astype(F32)
    kf = k_ref[...].astype(F32) * scale
    vb = v_ref[...]
    gc = gc_ref[...]
    total = None
    for d, row_ref in ((0, rowf_ref), (1, rowb_ref)):
        j = d * 8 + hd
        hh, (c_new, n_new, m_new) = _mlstm_chunk(
            qf, kf, vb, _lane_pick(gc, j * 8 + 1), _lane_pick(gc, j * 8 + 2),
            row_ref[0:1, :], None, reverse=(d == 1))
        total = hh if total is None else total + hh
        c_ref[d] = c_new
        n_ref[d, pl.ds(hd, 1), :] = n_new
        m_ref[d] = m_new
    h_ref[...] = total


def mlstm_prompt(z, gcol, grow, geo, heads, dk, dv):
    L = geo.tp
    kq = heads
    vo = 2 * heads * dk // dv
    return pl.pallas_call(
        functools.partial(_mlstm_prompt_kernel, scale=dk ** -0.5),
        grid=(geo.bp, heads),
        in_specs=[
            pl.BlockSpec((L, dk), lambda b, h: (b, h)),
            pl.BlockSpec((L, dk), lambda b, h: (b, kq + h)),
            pl.BlockSpec((L, dv), lambda b, h: (b, vo + h)),
            pl.BlockSpec((L, V7X_LANES), lambda b, h: (b, 0)),
            pl.BlockSpec((None, V7X_SUBLANES, L), lambda b, h: (b, h, 0)),
            pl.BlockSpec((None, V7X_SUBLANES, L), lambda b, h: (b, heads + h, 0)),
        ],
        out_specs=[
            pl.BlockSpec((L, dv), lambda b, h: (b, h)),
            pl.BlockSpec((None, 2, None, dk, dv), lambda b, h: (b, 0, h, 0, 0)),
            pl.BlockSpec((None, 2, heads, dk), lambda b, h: (b, 0, 0, 0)),
            pl.BlockSpec((None, 2, None, 1, 1), lambda b, h: (b, 0, h, 0, 0)),
        ],
        out_shape=[
            jax.ShapeDtypeStruct((geo.n, heads * dv), F32),
            jax.ShapeDtypeStruct((geo.bp, 2, heads, dk, dv), F32),
            jax.ShapeDtypeStruct((geo.bp, 2, heads, dk), F32),
            jax.ShapeDtypeStruct((geo.bp, 2, heads, 1, 1), F32),
        ],
        compiler_params=_cparams(("arbitrary", "arbitrary")),
    )(z, z, z, gcol, grow, grow)


def _mlstm_sample_kernel(q_ref, k_ref, v_ref, gc_ref, rowf_ref, rowb_ref, c0_ref, n0_ref, m0_ref,
                         _, h_ref, c_sc, n_sc, m_sc, *, scale, L, nc, heads):
    hd = pl.program_id(1)
    for d, row_ref in ((0, rowf_ref), (1, rowb_ref)):
        c_sc[...] = c0_ref[d]
        n_sc[...] = n0_ref[d, pl.ds(hd, 1), :]
        m_sc[...] = m0_ref[d]
        j = d * heads + hd

        def body(i, carry, d=d, row_ref=row_ref, j=j):
            c = (nc - 1 - i) if d == 1 else i
            rows = pl.ds(pl.multiple_of(c * L, L), L)
            gc = gc_ref[rows, :]
            hh, (c_new, n_new, m_new) = _mlstm_chunk(
                q_ref[rows, :].astype(F32), k_ref[rows, :].astype(F32) * scale, v_ref[rows, :],
                _lane_pick(gc, j * 8 + 1), _lane_pick(gc, j * 8 + 2), row_ref[c, 0:1, :],
                (c_sc[...], n_sc[...], m_sc[...]), reverse=(d == 1))
            c_sc[...] = c_new
            n_sc[...] = n_new
            m_sc[...] = m_new
            if d == 0:
                h_ref[rows, :] = hh
            else:
                h_ref[rows, :] += hh
            return carry

        lax.fori_loop(0, nc, body, 0)


def mlstm_sample(z, gcol, grow, state_c, state_n, state_m5, jl, geo, heads, dk, dv, L, hbuf):
    nc = geo.ts // L
    offs = geo.np_ // geo.ts
    kq = heads
    vo = 2 * heads * dk // dv
    ts = geo.ts
    in_specs = [
        pl.BlockSpec((ts, dk), lambda b, h: (offs + b, h)),
        pl.BlockSpec((ts, dk), lambda b, h: (offs + b, kq + h)),
        pl.BlockSpec((ts, dv), lambda b, h: (offs + b, vo + h)),
        pl.BlockSpec((ts, V7X_LANES), lambda b, h: (offs + b, 0)),
        pl.BlockSpec((nc, V7X_SUBLANES, L), lambda b, h: (offs + b, h, 0)),
        pl.BlockSpec((nc, V7X_SUBLANES, L), lambda b, h: (offs + b, heads + h, 0)),
        pl.BlockSpec((None, None, 2, None, dk, dv), lambda b, h: (b, jl, 0, h, 0, 0)),
        pl.BlockSpec((None, None, 2, heads, dk), lambda b, h: (b, jl, 0, 0, 0)),
        pl.BlockSpec((None, None, 2, None, 1, 1), lambda b, h: (b, jl, 0, h, 0, 0)),
        pl.BlockSpec(memory_space=pl.ANY),
    ]
    return pl.pallas_call(
        functools.partial(_mlstm_sample_kernel, scale=dk ** -0.5, L=L, nc=nc, heads=heads),
        grid=(geo.bs, heads),
        in_specs=in_specs,
        out_specs=pl.BlockSpec((ts, dv), lambda b, h: (offs + b, h)),
        out_shape=jax.ShapeDtypeStruct(hbuf.shape, F32),
        scratch_shapes=[pltpu.VMEM((dk, dv), F32), pltpu.VMEM((1, dk), F32), pltpu.VMEM((1, 1), F32)],
        input_output_aliases={9: 0},
        compiler_params=_cparams(("arbitrary", "arbitrary")),
    )(z, z, z, gcol, grow, grow, state_c, state_n, state_m5, hbuf)


def _mlstm_out_kernel(h_ref, o_ref, g_ref, out_ref, *, heads, dv):
    for hd in range(heads):
        sl = slice(hd * dv, (hd + 1) * dv)
        y = _rms(h_ref[:, sl]) * g_ref[hd:hd + 1, :]
        out_ref[:, sl] = (y * jax.nn.sigmoid(o_ref[:, sl].astype(F32))).astype(BF16)


def mlstm_out(hsum, z, g_head3, jl, heads, dv, tm=256):
    n, d = hsum.shape
    ocol = (z.shape[1] - d) // d
    return pl.pallas_call(
        functools.partial(_mlstm_out_kernel, heads=heads, dv=dv),
        grid=(n // tm,),
        in_specs=[
            pl.BlockSpec((tm, d), lambda i: (i, 0)),
            pl.BlockSpec((tm, d), lambda i: (i, ocol)),
            pl.BlockSpec((None, heads, dv), lambda i: (jl, 0, 0)),
        ],
        out_specs=pl.BlockSpec((tm, d), lambda i: (i, 0)),
        out_shape=jax.ShapeDtypeStruct((n, d), BF16),
        compiler_params=_cparams(("arbitrary",)),
    )(hsum, z, g_head3)


def _rope_kernel(q_ref, k_ref, c_ref, s_ref, qo_ref, ko_ref):
    cos = c_ref[...]
    sin = s_ref[...]
    lane = lax.broadcasted_iota(I32, cos.shape, 1)
    low = (lane & 32) == 0
    for ref, oref in ((q_ref, qo_ref), (k_ref, ko_ref)):
        for g in range(ref.shape[1] // V7X_LANES):
            sl = slice(g * V7X_LANES, (g + 1) * V7X_LANES)
            x = ref[:, sl]
            partner = jnp.where(low, pltpu.roll(x, V7X_LANES - 32, 1), pltpu.roll(x, 32, 1))
            oref[:, sl] = (x * cos + partner * sin).astype(BF16)


def rope_tables(ts, dh):
    half = dh // 2
    t = jnp.arange(ts)
    row = (t // GRID_W).astype(F32)
    col = (t % GRID_W).astype(F32)
    inv = ROPE_THETA ** (-jnp.arange(0, half, 2, dtype=F32) / half)
    ar = row[:, None] * inv
    ac = col[:, None] * inv
    cos = jnp.concatenate([jnp.cos(ar), jnp.cos(ar), jnp.cos(ac), jnp.cos(ac)], axis=-1)
    sin = jnp.concatenate([-jnp.sin(ar), jnp.sin(ar), -jnp.sin(ac), jnp.sin(ac)], axis=-1)
    return cos, sin


def rope_sample(qkv, cos, sin, geo, d, tm):
    off = geo.np_ // tm
    per = geo.ts // tm
    return pl.pallas_call(
        _rope_kernel,
        grid=(geo.ns // tm,),
        in_specs=[
            pl.BlockSpec((tm, d), lambda i: (off + i, 0)),
            pl.BlockSpec((tm, d), lambda i: (off + i, 1)),
            pl.BlockSpec((tm, V7X_LANES), lambda i: (i % per, 0)),
            pl.BlockSpec((tm, V7X_LANES), lambda i: (i % per, 0)),
        ],
        out_specs=[pl.BlockSpec((tm, d), lambda i: (i, 0)),
                   pl.BlockSpec((tm, d), lambda i: (i, 0))],
        out_shape=[jax.ShapeDtypeStruct((geo.ns, d), BF16),
                   jax.ShapeDtypeStruct((geo.ns, d), BF16)],
        compiler_params=_cparams(("arbitrary",)),
    )(qkv, qkv, cos, sin)


def _attn_kernel(*refs, dh, lam_init, cached):
    if cached:
        (lam_ref, q_ref, k_ref, v_ref, kc_ref, vc_ref, g_ref, _, o_ref,
         kk_sc, vv_sc, s_sc, e_sc) = refs
        new_kv = pl.program_id(2) == 0
    else:
        lam_ref, q_ref, k_ref, v_ref, g_ref, o_ref, kk_sc, vv_sc, s_sc, e_sc = refs
        new_kv = True
    s_new = k_ref.shape[0]

    @pl.when(new_kv)
    def _():
        kk_sc[0:s_new, :] = k_ref[...].astype(BF16)
        vv_sc[0:s_new, :] = v_ref[...].astype(BF16)
        if cached:
            kk_sc[s_new:, :] = kc_ref[...].astype(BF16)
            vv_sc[s_new:, :] = vc_ref[...].astype(BF16)

    lp = lam_ref[...]
    lam = (jnp.exp(jnp.sum(lp[0:1] * lp[1:2], axis=1, keepdims=True))
           - jnp.exp(jnp.sum(lp[2:3] * lp[3:4], axis=1, keepdims=True)) + lam_init)
    c2 = dh ** -0.5 * math.log2(math.e)
    tq = q_ref.shape[0]
    ck = V7X_LANES
    n_ck = kk_sc.shape[0] // ck
    qs = [q_ref[:, c * dh:(c + 1) * dh].astype(BF16) for c in range(2)]

    def scores(c, j):
        s = lax.dot_general(qs[c], kk_sc[j * ck:(j + 1) * ck, c * dh:(c + 1) * dh], NT_DIMS,
                            preferred_element_type=F32)
        s_sc[c, :, j * ck:(j + 1) * ck] = s
        return s

    def probs(c, j, m):
        e = jnp.exp2((s_sc[c, :, j * ck:(j + 1) * ck] - m) * c2)
        e_sc[c, :, j * ck:(j + 1) * ck] = e.astype(BF16)
        return e

    def pv(c, lv):
        l = jnp.sum(lv, axis=1, keepdims=True)
        return jnp.dot(e_sc[c], vv_sc[...], preferred_element_type=F32) * (1.0 / l)

    ninf = jnp.full((tq, ck), -jnp.inf, F32)
    zero = jnp.zeros((tq, ck), F32)
    mv0, mv1, lv0, lv1 = ninf, ninf, zero, zero
    for j in range(n_ck):
        mv0 = jnp.maximum(mv0, scores(0, j))
    m0 = jnp.max(mv0, axis=1, keepdims=True)
    for j in range(n_ck):
        lv0 = lv0 + probs(0, j, m0)
        mv1 = jnp.maximum(mv1, scores(1, j))
    m1 = jnp.max(mv1, axis=1, keepdims=True)
    o0 = pv(0, lv0)
    for j in range(n_ck):
        lv1 = lv1 + probs(1, j, m1)
    o = o0 - lam * pv(1, lv1)
    o_ref[...] = (_rms(o) * g_ref[...] * (1.0 - lam_init)).astype(BF16)


def _attn_scratch(tq, s_total, w):
    return [pltpu.VMEM((s_total, w), BF16), pltpu.VMEM((s_total, w), BF16),
            pltpu.VMEM((2, tq, s_total), F32), pltpu.VMEM((2, tq, s_total), BF16)]


def attn_prompt(qkv, a_lam, g_sub4, jl, geo, heads, dh, lam_init, out_rows):
    t = geo.tp
    w = 2 * dh
    return pl.pallas_call(
        functools.partial(_attn_kernel, dh=dh, lam_init=lam_init, cached=False),
        grid=(geo.bp, heads),
        in_specs=[
            pl.BlockSpec((None, 4, dh), lambda b, h: (jl, 0, 0)),
            pl.BlockSpec((t, w), lambda b, h: (b, h)),
            pl.BlockSpec((t, w), lambda b, h: (b, heads + h)),
            pl.BlockSpec((t, w), lambda b, h: (b, 2 * heads + h)),
            pl.BlockSpec((None, None, 1, w), lambda b, h: (jl, h, 0, 0)),
        ],
        out_specs=pl.BlockSpec((t, w), lambda b, h: (b, h)),
        out_shape=jax.ShapeDtypeStruct((out_rows, heads * w), BF16),
        scratch_shapes=_attn_scratch(t, t, w),
        compiler_params=_cparams(("arbitrary", "arbitrary")),
    )(a_lam, qkv, qkv, qkv, g_sub4)


def attn_sample(q_rot, k_rot, qkv, cache_k4, cache_v4, a_lam, g_sub4, jl, geo, heads, dh,
                lam_init, prev, tq):
    w = 2 * dh
    nq = geo.ts // tq
    offq = geo.np_ // tq
    offs = geo.np_ // geo.ts
    past = cache_k4.shape[2]
    return pl.pallas_call(
        functools.partial(_attn_kernel, dh=dh, lam_init=lam_init, cached=True),
        grid=(geo.bs, heads, nq),
        in_specs=[
            pl.BlockSpec((None, 4, dh), lambda b, h, i: (jl, 0, 0)),
            pl.BlockSpec((tq, w), lambda b, h, i: (b * nq + i, h)),
            pl.BlockSpec((geo.ts, w), lambda b, h, i: (b, h)),
            pl.BlockSpec((geo.ts, w), lambda b, h, i: (offs + b, 2 * heads + h)),
            pl.BlockSpec((None, None, past, w), lambda b, h, i: (b, jl, 0, h)),
            pl.BlockSpec((None, None, past, w), lambda b, h, i: (b, jl, 0, h)),
            pl.BlockSpec((None, None, 1, w), lambda b, h, i: (jl, h, 0, 0)),
            pl.BlockSpec(memory_space=pl.ANY),
        ],
        out_specs=pl.BlockSpec((tq, w), lambda b, h, i: (offq + b * nq + i, h)),
        out_shape=jax.ShapeDtypeStruct(prev.shape, BF16),
        input_output_aliases={7: 0},
        scratch_shapes=_attn_scratch(tq, geo.ts + past, w),
        compiler_params=_cparams(("arbitrary", "arbitrary", "arbitrary")),
    )(a_lam, q_rot, k_rot, qkv, cache_k4, cache_v4, g_sub4, prev)


def _router_kernel(x_ref, g_ref, sh_ref, sc_ref, rwh_ref, rwl_ref, rb_ref,
                   h_ref, ei_ref, rk_ref, gwc_ref, cnt_ref, carry_sc, *, n_exp):
    i = pl.program_id(0)

    @pl.when(i == 0)
    def _():
        carry_sc[...] = jnp.zeros_like(carry_sc)

    y = _rms(x_ref[...]) * g_ref[...]
    hm = y * (1.0 + sc_ref[...]) + sh_ref[...]
    h_ref[...] = hm
    tm = hm.shape[0]
    hi = hm.astype(BF16)
    lo = (hm - hi.astype(F32)).astype(BF16)
    rwh = rwh_ref[...]
    logits = (lax.dot_general(rwh, hi, NT_DIMS, preferred_element_type=F32)
              + lax.dot_general(rwh, lo, NT_DIMS, preferred_element_type=F32)
              + lax.dot_general(rwl_ref[...], hi, NT_DIMS, preferred_element_type=F32))
    ex = jnp.exp(logits - jnp.max(logits, axis=0, keepdims=True))
    probs = ex / jnp.sum(ex, axis=0, keepdims=True)
    sel = probs + rb_ref[...]
    epg = n_exp // N_GROUPS
    sub = lax.broadcasted_iota(I32, (epg, tm), 0).astype(F32)
    ninf = -jnp.inf

    def top2(v):
        m1 = jnp.max(v, axis=0, keepdims=True)
        a1 = jnp.min(jnp.where(v == m1, sub, float(epg)), axis=0, keepdims=True)
        v2 = jnp.where(sub == a1, ninf, v)
        m2 = jnp.max(v2, axis=0, keepdims=True)
        a2 = jnp.min(jnp.where(v2 == m2, sub, float(epg)), axis=0, keepdims=True)
        return m1, a1, m2, a2

    best = None
    gi = None
    for g in range(N_GROUPS):
        m1, _, m2, _ = top2(sel[g * epg:(g + 1) * epg, :])
        score = m1 + m2
        if g == 0:
            best, gi = score, jnp.zeros((1, tm), F32)
        else:
            upd = score > best
            best = jnp.where(upd, score, best)
            gi = jnp.where(upd, float(g), gi)
    ing = jnp.zeros((epg, tm), F32)
    pin = jnp.zeros((epg, tm), F32)
    for g in range(N_GROUPS):
        pick = gi == float(g)
        ing = jnp.where(pick, sel[g * epg:(g + 1) * epg, :], ing)
        pin = jnp.where(pick, probs[g * epg:(g + 1) * epg, :], pin)
    _, l1, _, l2 = top2(ing)
    w1 = jnp.sum(jnp.where(sub == l1, pin, 0.0), axis=0, keepdims=True)
    w2 = jnp.sum(jnp.where(sub == l2, pin, 0.0), axis=0, keepdims=True)
    ws = w1 + w2
    e1 = (gi * epg + l1).astype(I32)
    e2 = (gi * epg + l2).astype(I32)

    eio = lax.broadcasted_iota(I32, (n_exp, tm), 0)
    is1 = eio == e1
    is2 = eio == e2
    mem = jnp.where(is1 | is2, 1.0, 0.0)
    tri = jnp.where(lax.broadcasted_iota(I32, (tm, tm), 0) < lax.broadcasted_iota(I32, (tm, tm), 1),
                    1.0, 0.0).astype(BF16)
    pre = jnp.dot(mem.astype(BF16), tri, preferred_element_type=F32) + carry_sc[...]
    r1 = jnp.sum(jnp.where(is1, pre, 0.0), axis=0, keepdims=True)
    r2 = jnp.sum(jnp.where(is2, pre, 0.0), axis=0, keepdims=True)
    carry = carry_sc[...] + jnp.sum(mem, axis=1, keepdims=True)
    carry_sc[...] = carry
    cnt_ref[...] = jnp.broadcast_to(carry, cnt_ref.shape)

    row8 = lax.broadcasted_iota(I32, (V7X_SUBLANES, tm), 0)
    ei_ref[...] = jnp.where(row8 == 0, e1, jnp.where(row8 == 1, e2, 0))
    rk_ref[...] = jnp.where(row8 == 0, r1.astype(I32), jnp.where(row8 == 1, r2.astype(I32), 0))
    row128 = lax.broadcasted_iota(I32, (V7X_LANES, tm), 0)
    gw = jnp.where(row128 == 0, w1 / ws, jnp.where(row128 == 1, w2 / ws, 0.0))
    gwc_ref[...] = gw.T


def moe_router(x, g_norm4, mod, rwt_hi, rwt_lo, rb_col, geo, layer, tm=256):
    n, d = x.shape
    n_exp = rwt_hi.shape[0]
    return pl.pallas_call(
        functools.partial(_router_kernel, n_exp=n_exp),
        grid=(n // tm,),
        in_specs=[
            pl.BlockSpec((tm, d), lambda i: (i, 0)),
            pl.BlockSpec((None, None, 1, d), lambda i: (layer, 1, 0, 0)),
            pl.BlockSpec((None, None, None, 1, d), lambda i: (layer, geo.mod_row(i, tm), 3, 0, 0)),
            pl.BlockSpec((None, None, None, 1, d), lambda i: (layer, geo.mod_row(i, tm), 4, 0, 0)),
            pl.BlockSpec((n_exp, d), lambda i: (0, 0)),
            pl.BlockSpec((n_exp, d), lambda i: (0, 0)),
            pl.BlockSpec((n_exp, 1), lambda i: (0, 0)),
        ],
        out_specs=[
            pl.BlockSpec((tm, d), lambda i: (i, 0)),
            pl.BlockSpec((V7X_SUBLANES, tm), lambda i: (0, i)),
            pl.BlockSpec((V7X_SUBLANES, tm), lambda i: (0, i)),
            pl.BlockSpec((tm, V7X_LANES), lambda i: (i, 0)),
            pl.BlockSpec((n_exp, V7X_LANES), lambda i: (0, 0)),
        ],
        out_shape=[
            jax.ShapeDtypeStruct((n, d), F32),
            jax.ShapeDtypeStruct((V7X_SUBLANES, n), I32),
            jax.ShapeDtypeStruct((V7X_SUBLANES, n), I32),
            jax.ShapeDtypeStruct((n, V7X_LANES), F32),
            jax.ShapeDtypeStruct((n_exp, V7X_LANES), F32),
        ],
        scratch_shapes=[pltpu.VMEM((n_exp, 1), F32)],
        compiler_params=_cparams(("arbitrary",)),
    )(x, g_norm4, mod, mod, rwt_hi, rwt_lo, rb_col)


def _gather_kernel(tok_ref, nused_ref, h_hbm, o_ref, buf, sem, *, bm):
    blk = pl.program_id(0)
    n_used = nused_ref[0]

    def row_copy(b, r):
        slot = b % 2
        return pltpu.make_async_copy(h_hbm.at[pl.ds(tok_ref[b * bm + r], 1)],
                                     buf.at[slot, pl.ds(r, 1)], sem.at[slot])

    def start_block(b):
        def start(r, carry):
            row_copy(b, r).start()
            return carry
        lax.fori_loop(0, bm, start, 0, unroll=8)

    @pl.when(blk == 0)
    def _():
        start_block(blk)

    @pl.when(blk + 1 < n_used)
    def _():
        start_block(blk + 1)

    @pl.when(blk < n_used)
    def _():
        def wait(r, carry):
            row_copy(blk, r).wait()
            return carry
        lax.fori_loop(0, bm, wait, 0, unroll=8)
        o_ref[...] = buf[blk % 2].astype(BF16)


def moe_gather(h2, slot_tok, n_used, n_blocks, bm):
    n, d = h2.shape
    return pl.pallas_call(
        functools.partial(_gather_kernel, bm=bm),
        grid_spec=pltpu.PrefetchScalarGridSpec(
            num_scalar_prefetch=2,
            grid=(n_blocks,),
            in_specs=[pl.BlockSpec(memory_space=pl.ANY)],
            out_specs=pl.BlockSpec((bm, d), lambda b, tok, nu: (b, 0)),
            scratch_shapes=[pltpu.VMEM((2, bm, d), F32), pltpu.SemaphoreType.DMA((2,))],
        ),
        out_shape=jax.ShapeDtypeStruct((n_blocks * bm, d), BF16),
        compiler_params=_cparams(("arbitrary",)),
    )(slot_tok, n_used, h2)


def _expert_blocks(nb, in_copy, out_copy, compute):
    in_copy(0).start()

    def body(i, carry):
        slot = i % 2

        @pl.when(i + 1 < nb)
        def _():
            in_copy(i + 1).start()

        in_copy(i).wait()

        @pl.when(i >= 2)
        def _():
            out_copy(i - 2).wait()

        compute(slot)
        out_copy(i).start()
        return carry

    lax.fori_loop(0, nb, body, 0)

    @pl.when(nb >= 2)
    def _():
        out_copy(nb - 2).wait()

    out_copy(nb - 1).wait()


def _moe_up_kernel(b0_ref, nb_ref, wg_ref, wu_ref, xs_hbm, a_hbm, wg_sc, wu_sc, xbuf, abuf,
                   sem_in, sem_out, *, bm, tf):
    e = pl.program_id(0)
    f = pl.program_id(1)
    nb = nb_ref[e]
    b0 = b0_ref[e]

    def in_copy(i):
        return pltpu.make_async_copy(xs_hbm.at[pl.ds((b0 + i) * bm, bm)], xbuf.at[i % 2],
                                     sem_in.at[i % 2])

    def out_copy(i):
        return pltpu.make_async_copy(abuf.at[i % 2],
                                     a_hbm.at[pl.ds((b0 + i) * bm, bm), pl.ds(f * tf, tf)],
                                     sem_out.at[i % 2])

    def compute(slot):
        x = xbuf[slot]
        g = jnp.dot(x, wg_sc[...], preferred_element_type=F32)
        u = jnp.dot(x, wu_sc[...], preferred_element_type=F32)
        abuf[slot] = (_silu(g) * u).astype(BF16)

    @pl.when(nb > 0)
    def _():
        wg_sc[...] = wg_ref[...].astype(BF16)
        wu_sc[...] = wu_ref[...].astype(BF16)
        _expert_blocks(nb, in_copy, out_copy, compute)


def moe_up(xs, b0_e, nb_e, w_gate, w_up, layer, bm, tf):
    n_slots, d = xs.shape
    n_exp, ff = w_gate.shape[1], w_gate.shape[3]
    wspec = pl.BlockSpec((None, None, d, tf), lambda e, f, b0, nb: (layer, e, 0, f))
    return pl.pallas_call(
        functools.partial(_moe_up_kernel, bm=bm, tf=tf),
        grid_spec=pltpu.PrefetchScalarGridSpec(
            num_scalar_prefetch=2,
            grid=(n_exp, ff // tf),
            in_specs=[wspec, wspec, pl.BlockSpec(memory_space=pl.ANY)],
            out_specs=pl.BlockSpec(memory_space=pl.ANY),
            scratch_shapes=[pltpu.VMEM((d, tf), BF16), pltpu.VMEM((d, tf), BF16),
                            pltpu.VMEM((2, bm, d), BF16), pltpu.VMEM((2, bm, tf), BF16),
                            pltpu.SemaphoreType.DMA((2,)), pltpu.SemaphoreType.DMA((2,))],
        ),
        out_shape=jax.ShapeDtypeStruct((n_slots, ff), BF16),
        compiler_params=_cparams(("arbitrary", "arbitrary")),
    )(b0_e, nb_e, w_gate, w_up, xs)


def _moe_down_kernel(b0_ref, nb_ref, wd_ref, a_hbm, y_hbm, wd_sc, abuf, ybuf, sem_in, sem_out, *, bm):
    e = pl.program_id(0)
    nb = nb_ref[e]
    b0 = b0_ref[e]

    def in_copy(i):
        return pltpu.make_async_copy(a_hbm.at[pl.ds((b0 + i) * bm, bm)], abuf.at[i % 2],
                                     sem_in.at[i % 2])

    def out_copy(i):
        return pltpu.make_async_copy(ybuf.at[i % 2], y_hbm.at[pl.ds((b0 + i) * bm, bm)],
                                     sem_out.at[i % 2])

    def compute(slot):
        ybuf[slot] = jnp.dot(abuf[slot], wd_sc[...], preferred_element_type=F32)

    @pl.when(nb > 0)
    def _():
        wd_sc[...] = wd_ref[...].astype(BF16)
        _expert_blocks(nb, in_copy, out_copy, compute)


def moe_down(act, b0_e, nb_e, w_down, layer, bm):
    n_slots, ff = act.shape
    n_exp, d = w_down.shape[1], w_down.shape[3]
    return pl.pallas_call(
        functools.partial(_moe_down_kernel, bm=bm),
        grid_spec=pltpu.PrefetchScalarGridSpec(
            num_scalar_prefetch=2,
            grid=(n_exp,),
            in_specs=[pl.BlockSpec((None, None, ff, d), lambda e, b0, nb: (layer, e, 0, 0)),
                      pl.BlockSpec(memory_space=pl.ANY)],
            out_specs=pl.BlockSpec(memory_space=pl.ANY),
            scratch_shapes=[pltpu.VMEM((ff, d), BF16), pltpu.VMEM((2, bm, ff), BF16),
                            pltpu.VMEM((2, bm, d), F32),
                            pltpu.SemaphoreType.DMA((2,)), pltpu.SemaphoreType.DMA((2,))],
        ),
        out_shape=jax.ShapeDtypeStruct((n_slots, d), F32),
        compiler_params=_cparams(("arbitrary",)),
    )(b0_e, nb_e, w_down, act)


def _combine_kernel(dest_ref, x_ref, g_ref, gwc_ref, y_hbm, o_ref, buf, sem, *, tb, n_tok):
    i = pl.program_id(0)

    def row_copy(blk, k, r):
        slot = blk % 2
        return pltpu.make_async_copy(y_hbm.at[pl.ds(dest_ref[k * n_tok + blk * tb + r], 1)],
                                     buf.at[slot, k, pl.ds(r, 1)], sem.at[slot])

    def start_block(blk):
        def start(r, carry):
            row_copy(blk, 0, r).start()
            row_copy(blk, 1, r).start()
            return carry
        lax.fori_loop(0, tb, start, 0, unroll=8)

    @pl.when(i == 0)
    def _():
        start_block(i)

    @pl.when(i + 1 < pl.num_programs(0))
    def _():
        start_block(i + 1)

    def wait(r, carry):
        row_copy(i, 0, r).wait()
        row_copy(i, 1, r).wait()
        return carry

    lax.fori_loop(0, tb, wait, 0, unroll=8)
    slot = i % 2
    gwc = gwc_ref[...]
    y = gwc[:, 0:1] * buf[slot, 0] + gwc[:, 1:2] * buf[slot, 1]
    o_ref[...] = x_ref[...] + g_ref[...] * y


def moe_combine(x, yb, dest, gwc, mod, geo, layer, tb=256):
    n, d = x.shape
    return pl.pallas_call(
        functools.partial(_combine_kernel, tb=tb, n_tok=n),
        grid_spec=pltpu.PrefetchScalarGridSpec(
            num_scalar_prefetch=1,
            grid=(n // tb,),
            in_specs=[
                pl.BlockSpec((tb, d), lambda i, dr: (i, 0)),
                pl.BlockSpec((None, None, None, 1, d), lambda i, dr: (layer, geo.mod_row(i, tb), 5, 0, 0)),
                pl.BlockSpec((tb, V7X_LANES), lambda i, dr: (i, 0)),
                pl.BlockSpec(memory_space=pl.ANY),
            ],
            out_specs=pl.BlockSpec((tb, d), lambda i, dr: (i, 0)),
            scratch_shapes=[pltpu.VMEM((2, 2, tb, d), F32), pltpu.SemaphoreType.DMA((2,))],
        ),
        out_shape=jax.ShapeDtypeStruct((n, d), F32),
        compiler_params=_cparams(("arbitrary",)),
    )(dest, x, mod, gwc, yb)


def moe_layer(x, g_norm4, mod, rwt_hi, rwt_lo, rb_col, w_gate, w_up, w_down, geo, layer):
    n, d = x.shape
    n_exp = rwt_hi.shape[0]
    bm = MOE_BM
    h2, ei, rk, gwc, cnt = moe_router(x, g_norm4, mod, rwt_hi, rwt_lo, rb_col, geo, layer)
    counts = cnt[:, 0].astype(I32)
    nb_e = (counts + bm - 1) // bm
    b0_e = jnp.cumsum(nb_e) - nb_e
    pad_start = b0_e * bm
    e_flat = ei[:2].reshape(-1)
    dest = pad_start[e_flat] + rk[:2].reshape(-1)
    n_blocks = (2 * n) // bm + n_exp
    tok = jnp.tile(jnp.arange(n, dtype=I32), 2)
    slot_tok = jnp.zeros((n_blocks * bm,), I32).at[dest].set(tok)
    n_used = jnp.sum(nb_e).astype(I32).reshape(1)
    xs = moe_gather(h2, slot_tok, n_used, n_blocks, bm)
    ff = w_gate.shape[3]
    tf = min(MOE_TF, ff)
    act = moe_up(xs, b0_e, nb_e, w_gate, w_up, layer, bm, tf)
    yb = moe_down(act, b0_e, nb_e, w_down, layer, bm)
    return moe_combine(x, yb, dest, gwc, mod, geo, layer)


def _final_kernel(x_ref, g_ref, o_ref):
    o_ref[...] = _rms(x_ref[...]) * g_ref[...]


def final_norm(x, g_row, row0, rows, tm=256):
    d = x.shape[1]
    off = row0 // tm
    return pl.pallas_call(
        _final_kernel,
        grid=(rows // tm,),
        in_specs=[pl.BlockSpec((tm, d), lambda i: (off + i, 0)),
                  pl.BlockSpec((1, d), lambda i: (0, 0))],
        out_specs=pl.BlockSpec((tm, d), lambda i: (i, 0)),
        out_shape=jax.ShapeDtypeStruct((rows, d), F32),
        compiler_params=_cparams(("arbitrary",)),
    )(x, g_row)


def _gate_weights(w_in, b_gate, n_main, heads):
    j = jnp.arange(V7X_LANES) // V7X_SUBLANES
    d_, h_ = j // heads, j % heads
    ci = n_main + d_ * 2 * heads + h_
    cf = ci + heads
    wi = w_in[:, ci].astype(BF16)
    wf = w_in[:, cf].astype(BF16)
    bi = b_gate[ci - n_main].reshape(1, -1)
    bf = b_gate[cf - n_main].reshape(1, -1)
    return wi, wf, bi, bf


def kernel(x_prompt, x_sample, state_C, state_n, state_m, cache_k, cache_v, c, c_ctx,
           w_mod, b_mod, g_norm, m_w_in, m_b_gate, m_g_head, m_w_out,
           a_w_qkv, a_lam, a_g_sub, a_w_o, router_w, router_b, w_gate, w_up, w_down, g_final):
    bp, tp, d = x_prompt.shape
    bs, ts, _ = x_sample.shape
    depth = w_mod.shape[0]
    geo = Geo(bp, tp, bs, ts)
    tm = geo.tm
    m_heads, m_dv = m_g_head.shape[1], m_g_head.shape[2]
    m_dk = m_dv // 2
    a_heads = a_g_sub.shape[1]
    a_dh = a_g_sub.shape[2] // 2
    n_exp = router_w.shape[1]
    assert tp == MLSTM_CHUNK and ts % MLSTM_CHUNK == 0 and geo.np_ % ts == 0
    assert 2 * m_heads * V7X_SUBLANES == V7X_LANES and n_exp // N_GROUPS == V7X_SUBLANES

    x = jnp.concatenate([x_prompt.reshape(geo.np_, d), x_sample.reshape(geo.ns, d)], axis=0)

    n_rows = 2 * V7X_SUBLANES
    cv = jnp.zeros((n_rows, d), F32).at[0].set(c_ctx).at[1:1 + bs].set(c)
    mod = adaln_all(cv, w_mod, b_mod).reshape(depth, n_rows, 6, 1, d)
    g_norm4 = g_norm.reshape(depth, 2, 1, d)

    rwt = router_w.T
    rwt_hi = rwt.astype(BF16)
    rwt_lo = (rwt - rwt_hi.astype(F32)).astype(BF16)
    rb_col = router_b.reshape(n_exp, 1)

    new_c, new_n, new_m, new_k, new_v = [], [], [], [], []
    for i in range(depth):
        jl = i // N_MIXERS
        h = modnorm(x, g_norm4, mod, geo, i, 0, 0, 1)
        if i % N_MIXERS == 0:
            n_main = 2 * m_heads * m_dk + 2 * m_heads * m_dv
            z = matmul(h, m_w_in, jl, n_main, tm, out_dtype=BF16)
            wi, wf, bi, bf = _gate_weights(m_w_in[jl], m_b_gate[jl], n_main, m_heads)
            L = MLSTM_CHUNK
            gcol, grow = mlstm_gates(h, wi, wf, bi, bf, L, tm)
            hsum, c_new, n_new, m_new = mlstm_prompt(z, gcol, grow, geo, m_heads, m_dk, m_dv)
            new_c.append(c_new)
            new_n.append(n_new)
            new_m.append(m_new.reshape(bp, 2, m_heads))
            state_m5 = state_m.reshape(state_m.shape + (1, 1))
            hsum = mlstm_sample(z, gcol, grow, state_C, state_n, state_m5, jl, geo, m_heads,
                                m_dk, m_dv, L, hsum)
            mix_in = mlstm_out(hsum, z, m_g_head, jl, m_heads, m_dv)
            x = matmul_residual(mix_in, m_w_out, jl, x, mod, geo, i, 2, tm)
        else:
            lam_init = 0.8 - 0.6 * math.exp(-0.3 * i)
            qkv = matmul(h, a_w_qkv, jl, 3 * d, tm)
            new_k.append(qkv[:geo.np_, d:2 * d].reshape(bp, tp, a_heads, 2, a_dh))
            new_v.append(qkv[:geo.np_, 2 * d:].reshape(bp, tp, a_heads, 2 * a_dh))
            g_sub4 = a_g_sub.reshape(a_g_sub.shape[0], a_heads, 1, 2 * a_dh)
            cos, sin = rope_tables(ts, a_dh)
            q_rot, k_rot = rope_sample(qkv, cos, sin, geo, d, min(512, ts))
            att = attn_prompt(qkv, a_lam, g_sub4, jl, geo, a_heads, a_dh, lam_init, geo.n)
            ck4 = cache_k.reshape(bs, cache_k.shape[1], cache_k.shape[2], d)
            cv4 = cache_v.reshape(bs, cache_v.shape[1], cache_v.shape[2], d)
            att = attn_sample(q_rot, k_rot, qkv, ck4, cv4, a_lam, g_sub4, jl, geo, a_heads, a_dh,
                              lam_init, att, min(ATTN_TQ, ts))
            x = matmul_residual(att, a_w_o, jl, x, mod, geo, i, 2, tm)
        x = moe_layer(x, g_norm4, mod, rwt_hi, rwt_lo, rb_col, w_gate, w_up, w_down, geo, i)

    g_row = g_final.reshape(1, d)
    y_prompt = final_norm(x, g_row, 0, geo.np_).reshape(bp, tp, d)
    y_sample = final_norm(x, g_row, geo.np_, geo.ns).reshape(bs, ts, d)
    return (y_prompt, y_sample,
            jnp.stack(new_c, axis=1), jnp.stack(new_n, axis=1), jnp.stack(new_m, axis=1),
            jnp.stack(new_k, axis=1), jnp.stack(new_v, axis=1))
```

```python
import functools
import math

import jax
import jax.numpy as jnp
from jax import lax
from jax.experimental import pallas as pl
from jax.experimental.pallas import tpu as pltpu

F32 = jnp.float32
BF16 = jnp.bfloat16
I32 = jnp.int32

EPS = 1e-6
N_GROUPS = 8
GRID_W = 64
ROPE_THETA = 10000.0
N_MIXERS = 2

V7X_LANES = 128
V7X_SUBLANES = 8
V7X_VMEM_LIMIT = 56 * 1024 * 1024

MLSTM_CHUNK = 256
MOE_BM = 256
MOE_TF = 512
ATTN_TQ = 512
ATTN_RT = 256

NT_DIMS = (((1,), (1,)), ((), ()))
TN_DIMS = (((0,), (0,)), ((), ()))


def _cparams(sem):
    return pltpu.CompilerParams(dimension_semantics=sem, vmem_limit_bytes=V7X_VMEM_LIMIT)


def _silu(x):
    return x * jax.nn.sigmoid(x)


def _rms(x):
    return x * lax.rsqrt(jnp.mean(x * x, axis=-1, keepdims=True) + EPS)


def _adaln_kernel(cv_ref, w_ref, b_ref, o_ref):
    s = _silu(cv_ref[...]).astype(BF16)
    o_ref[...] = jnp.dot(s, w_ref[...].astype(BF16), preferred_element_type=F32) + b_ref[...]


def adaln_all(cv, w_mod, b_mod, tn=512):
    depth, d, n6 = w_mod.shape
    r = cv.shape[0]
    return pl.pallas_call(
        _adaln_kernel,
        grid=(depth, n6 // tn),
        in_specs=[
            pl.BlockSpec((r, d), lambda l, j: (0, 0)),
            pl.BlockSpec((None, d, tn), lambda l, j: (l, 0, j)),
            pl.BlockSpec((None, 1, tn), lambda l, j: (l, 0, j)),
        ],
        out_specs=pl.BlockSpec((None, r, tn), lambda l, j: (l, 0, j)),
        out_shape=jax.ShapeDtypeStruct((depth, r, n6), F32),
        compiler_params=_cparams(("arbitrary", "arbitrary")),
    )(cv, w_mod, b_mod.reshape(depth, 1, n6))


class Geo:
    def __init__(self, bp, tp, bs, ts):
        self.bp, self.tp, self.bs, self.ts = bp, tp, bs, ts
        self.np_ = bp * tp
        self.ns = bs * ts
        self.n = self.np_ + self.ns
        self.tm = min(1024, math.gcd(self.np_, ts))

    def mod_row(self, i, tm):
        r0 = i * tm
        return jnp.where(r0 < self.np_, 0, 1 + (r0 - self.np_) // self.ts)


def _modnorm_kernel(x_ref, g_ref, sh_ref, sc_ref, o_ref):
    y = _rms(x_ref[...]) * g_ref[...]
    o_ref[...] = (y * (1.0 + sc_ref[...]) + sh_ref[...]).astype(o_ref.dtype)


def modnorm(x, g_norm4, mod, geo, layer, which_g, which_sh, which_sc, tm=256):
    n, d = x.shape
    return pl.pallas_call(
        _modnorm_kernel,
        grid=(n // tm,),
        in_specs=[
            pl.BlockSpec((tm, d), lambda i: (i, 0)),
            pl.BlockSpec((None, None, 1, d), lambda i: (layer, which_g, 0, 0)),
            pl.BlockSpec((None, None, None, 1, d), lambda i: (layer, geo.mod_row(i, tm), which_sh, 0, 0)),
            pl.BlockSpec((None, None, None, 1, d), lambda i: (layer, geo.mod_row(i, tm), which_sc, 0, 0)),
        ],
        out_specs=pl.BlockSpec((tm, d), lambda i: (i, 0)),
        out_shape=jax.ShapeDtypeStruct((n, d), BF16),
        compiler_params=_cparams(("arbitrary",)),
    )(x, g_norm4, mod, mod)


def _mm_kernel(a_ref, w_ref, o_ref):
    o_ref[...] = jnp.dot(a_ref[...], w_ref[...].astype(BF16),
                         preferred_element_type=F32).astype(o_ref.dtype)


def _mm_res_kernel(a_ref, w_ref, x_ref, g_ref, o_ref):
    acc = jnp.dot(a_ref[...], w_ref[...].astype(BF16), preferred_element_type=F32)
    o_ref[...] = x_ref[...] + g_ref[...] * acc


def matmul(a, w3, widx, n_cols, tm, tn=512, out_dtype=F32):
    m, k = a.shape
    return pl.pallas_call(
        _mm_kernel,
        grid=(m // tm, n_cols // tn),
        in_specs=[
            pl.BlockSpec((tm, k), lambda i, j: (i, 0)),
            pl.BlockSpec((None, k, tn), lambda i, j: (widx, 0, j)),
        ],
        out_specs=pl.BlockSpec((tm, tn), lambda i, j: (i, j)),
        out_shape=jax.ShapeDtypeStruct((m, n_cols), out_dtype),
        compiler_params=_cparams(("arbitrary", "arbitrary")),
    )(a, w3)


def matmul_residual(a, w3, widx, x, mod, geo, layer, which_gate, tm, tn=512):
    m, k = a.shape
    d = w3.shape[2]
    return pl.pallas_call(
        _mm_res_kernel,
        grid=(m // tm, d // tn),
        in_specs=[
            pl.BlockSpec((tm, k), lambda i, j: (i, 0)),
            pl.BlockSpec((None, k, tn), lambda i, j: (widx, 0, j)),
            pl.BlockSpec((tm, tn), lambda i, j: (i, j)),
            pl.BlockSpec((None, None, None, 1, tn),
                         lambda i, j: (layer, geo.mod_row(i, tm), which_gate, 0, j)),
        ],
        out_specs=pl.BlockSpec((tm, tn), lambda i, j: (i, j)),
        out_shape=jax.ShapeDtypeStruct((m, d), F32),
        compiler_params=_cparams(("arbitrary", "arbitrary")),
    )(a, w3, x, mod)


def _gates_kernel(h_ref, wi_ref, wf_ref, bi_ref, bf_ref, col_ref, row_ref, *, chunk):
    h = h_ref[...]
    tm = h.shape[0]
    gi = jnp.dot(h, wi_ref[...], preferred_element_type=F32) + bi_ref[...]
    xf = jnp.dot(h, wf_ref[...], preferred_element_type=F32) + bf_ref[...]
    lf = jnp.minimum(xf, 0.0) - jnp.log1p(jnp.exp(-jnp.abs(xf)))
    r = lax.broadcasted_iota(I32, (tm, V7X_LANES), 0) % chunk
    lane = lax.broadcasted_iota(I32, (tm, V7X_LANES), 1)
    pre = lf
    suf = lf
    s = 1
    while s < chunk:
        pre = pre + jnp.where(r >= s, pltpu.roll(pre, s, 0), 0.0)
        suf = suf + jnp.where(r < chunk - s, pltpu.roll(suf, tm - s, 0), 0.0)
        s *= 2
    b = jnp.where(lane < V7X_LANES // 2, pre, suf)
    sub = lane % V7X_SUBLANES
    col = jnp.where(sub == 0, gi - b, jnp.where(sub == 1, b, gi))
    col_ref[...] = col
    for ci in range(tm // chunk):
        row_ref[ci] = col[ci * chunk:(ci + 1) * chunk, :].T


def mlstm_gates(h, wi, wf, bi, bf, chunk, tm):
    n, d = h.shape
    return pl.pallas_call(
        functools.partial(_gates_kernel, chunk=chunk),
        grid=(n // tm,),
        in_specs=[
            pl.BlockSpec((tm, d), lambda i: (i, 0)),
            pl.BlockSpec((d, V7X_LANES), lambda i: (0, 0)),
            pl.BlockSpec((d, V7X_LANES), lambda i: (0, 0)),
            pl.BlockSpec((1, V7X_LANES), lambda i: (0, 0)),
            pl.BlockSpec((1, V7X_LANES), lambda i: (0, 0)),
        ],
        out_specs=[
            pl.BlockSpec((tm, V7X_LANES), lambda i: (i, 0)),
            pl.BlockSpec((tm // chunk, V7X_LANES, chunk), lambda i: (i, 0, 0)),
        ],
        out_shape=[jax.ShapeDtypeStruct((n, V7X_LANES), F32),
                   jax.ShapeDtypeStruct((n // chunk, V7X_LANES, chunk), F32)],
        compiler_params=_cparams(("arbitrary",)),
    )(h, wi, wf, bi, bf)


def _lane_pick(gc, lane_idx):
    lane = lax.broadcasted_iota(I32, gc.shape, 1)
    return jnp.sum(jnp.where(lane == lane_idx, gc, 0.0), axis=1, keepdims=True)


def _mlstm_chunk(qf, kf, vb, b_col, ig_col, a_row, state, reverse):
    L = qf.shape[0]
    qb = qf.astype(BF16)
    kb = kf.astype(BF16)
    ti = lax.broadcasted_iota(I32, (L, L), 0)
    si = lax.broadcasted_iota(I32, (L, L), 1)
    mask = (si >= ti) if reverse else (si <= ti)
    log_d = jnp.where(mask, b_col + a_row, -jnp.inf)
    mx = jnp.max(log_d, axis=1, keepdims=True)
    if state is None:
        m_old = jnp.zeros((1, 1), F32)
    else:
        c_old, n_old, m_old = state
    log_inter = b_col + m_old
    m_t = jnp.maximum(log_inter, mx)
    s = lax.dot_general(qb, kb, NT_DIMS, preferred_element_type=F32) * jnp.exp(log_d - m_t)
    num = jnp.dot(s.astype(BF16), vb, preferred_element_type=F32)
    den = jnp.sum(s, axis=1, keepdims=True)
    if state is not None:
        w_inter = jnp.exp(log_inter - m_t)
        num = num + w_inter * jnp.dot(qb, c_old.astype(BF16), preferred_element_type=F32)
        den = den + w_inter * jnp.sum(qf * n_old, axis=1, keepdims=True)
    h = num / jnp.maximum(jnp.abs(den), jnp.exp(-m_t))
    b_last = b_col[0:1, :] if reverse else b_col[L - 1:L, :]
    log_w = b_last - b_col + ig_col
    m_new = jnp.maximum(b_last + m_old, jnp.max(log_w, axis=0, keepdims=True))
    w_s = jnp.exp(log_w - m_new)
    kw = kf * w_s
    c_new = lax.dot_general(kw.astype(BF16), vb, TN_DIMS, preferred_element_type=F32)
    n_new = jnp.sum(kw, axis=0, keepdims=True)
    if state is not None:
        decay = jnp.exp(b_last + m_old - m_new)
        c_new = c_new + decay * c_old
        n_new = n_new + decay * n_old
    return h, (c_new, n_new, m_new)


def _mlstm_prompt_kernel(q_ref, k_ref, v_ref, gc_ref, rowf_ref, rowb_ref,
                         h_ref, c_ref, n_ref, m_ref, *, scale):
    hd = pl.program_id(1)
    qf = q_ref[...].astype(F32)
    kf = k_ref[...].astype(F32) * scale
    vb = v_ref[...]
    gc = gc_ref[...]
    total = None
    for d, row_ref in ((0, rowf_ref), (1, rowb_ref)):
        j = d * 8 + hd
        hh, (c_new, n_new, m_new) = _mlstm_chunk(
            qf, kf, vb, _lane_pick(gc, j * 8 + 1), _lane_pick(gc, j * 8 + 2),
            row_ref[0:1, :], None, reverse=(d == 1))
        total = hh if total is None else total + hh
        c_ref[d] = c_new
        n_ref[d, pl.ds(hd, 1), :] = n_new
        m_ref[d] = m_new
    h_ref[...] = total


def mlstm_prompt(z, gcol, grow, geo, heads, dk, dv):
    L = geo.tp
    kq = heads
    vo = 2 * heads * dk // dv
    return pl.pallas_call(
        functools.partial(_mlstm_prompt_kernel, scale=dk ** -0.5),
        grid=(geo.bp, heads),
        in_specs=[
            pl.BlockSpec((L, dk), lambda b, h: (b, h)),
            pl.BlockSpec((L, dk), lambda b, h: (b, kq + h)),
            pl.BlockSpec((L, dv), lambda b, h: (b, vo + h)),
            pl.BlockSpec((L, V7X_LANES), lambda b, h: (b, 0)),
            pl.BlockSpec((None, V7X_SUBLANES, L), lambda b, h: (b, h, 0)),
            pl.BlockSpec((None, V7X_SUBLANES, L), lambda b, h: (b, heads + h, 0)),
        ],
        out_specs=[
            pl.BlockSpec((L, dv), lambda b, h: (b, h)),
            pl.BlockSpec((None, 2, None, dk, dv), lambda b, h: (b, 0, h, 0, 0)),
            pl.BlockSpec((None, 2, heads, dk), lambda b, h: (b, 0, 0, 0)),
            pl.BlockSpec((None, 2, None, 1, 1), lambda b, h: (b, 0, h, 0, 0)),
        ],
        out_shape=[
            jax.ShapeDtypeStruct((geo.n, heads * dv), F32),
            jax.ShapeDtypeStruct((geo.bp, 2, heads, dk, dv), F32),
            jax.ShapeDtypeStruct((geo.bp, 2, heads, dk), F32),
            jax.ShapeDtypeStruct((geo.bp, 2, heads, 1, 1), F32),
        ],
        compiler_params=_cparams(("arbitrary", "arbitrary")),
    )(z, z, z, gcol, grow, grow)


def _mlstm_sample_kernel(q_ref, k_ref, v_ref, gc_ref, rowf_ref, rowb_ref, c0_ref, n0_ref, m0_ref,
                         _, h_ref, c_sc, n_sc, m_sc, *, scale, L, nc, heads):
    hd = pl.program_id(1)
    for d, row_ref in ((0, rowf_ref), (1, rowb_ref)):
        c_sc[...] = c0_ref[d]
        n_sc[...] = n0_ref[d, pl.ds(hd, 1), :]
        m_sc[...] = m0_ref[d]
        j = d * heads + hd

        def body(i, carry, d=d, row_ref=row_ref, j=j):
            c = (nc - 1 - i) if d == 1 else i
            rows = pl.ds(pl.multiple_of(c * L, L), L)
            gc = gc_ref[rows, :]
            hh, (c_new, n_new, m_new) = _mlstm_chunk(
                q_ref[rows, :].astype(F32), k_ref[rows, :].astype(F32) * scale, v_ref[rows, :],
                _lane_pick(gc, j * 8 + 1), _lane_pick(gc, j * 8 + 2), row_ref[c, 0:1, :],
                (c_sc[...], n_sc[...], m_sc[...]), reverse=(d == 1))
            c_sc[...] = c_new
            n_sc[...] = n_new
            m_sc[...] = m_new
            if d == 0:
                h_ref[rows, :] = hh
            else:
                h_ref[rows, :] += hh
            return carry

        lax.fori_loop(0, nc, body, 0)


def mlstm_sample(z, gcol, grow, state_c, state_n, state_m5, jl, geo, heads, dk, dv, L, hbuf):
    nc = geo.ts // L
    offs = geo.np_ // geo.ts
    kq = heads
    vo = 2 * heads * dk // dv
    ts = geo.ts
    in_specs = [
        pl.BlockSpec((ts, dk), lambda b, h: (offs + b, h)),
        pl.BlockSpec((ts, dk), lambda b, h: (offs + b, kq + h)),
        pl.BlockSpec((ts, dv), lambda b, h: (offs + b, vo + h)),
        pl.BlockSpec((ts, V7X_LANES), lambda b, h: (offs + b, 0)),
        pl.BlockSpec((nc, V7X_SUBLANES, L), lambda b, h: (offs + b, h, 0)),
        pl.BlockSpec((nc, V7X_SUBLANES, L), lambda b, h: (offs + b, heads + h, 0)),
        pl.BlockSpec((None, None, 2, None, dk, dv), lambda b, h: (b, jl, 0, h, 0, 0)),
        pl.BlockSpec((None, None, 2, heads, dk), lambda b, h: (b, jl, 0, 0, 0)),
        pl.BlockSpec((None, None, 2, None, 1, 1), lambda b, h: (b, jl, 0, h, 0, 0)),
        pl.BlockSpec(memory_space=pl.ANY),
    ]
    return pl.pallas_call(
        functools.partial(_mlstm_sample_kernel, scale=dk ** -0.5, L=L, nc=nc, heads=heads),
        grid=(geo.bs, heads),
        in_specs=in_specs,
        out_specs=pl.BlockSpec((ts, dv), lambda b, h: (offs + b, h)),
        out_shape=jax.ShapeDtypeStruct(hbuf.shape, F32),
        scratch_shapes=[pltpu.VMEM((dk, dv), F32), pltpu.VMEM((1, dk), F32), pltpu.VMEM((1, 1), F32)],
        input_output_aliases={9: 0},
        compiler_params=_cparams(("arbitrary", "arbitrary")),
    )(z, z, z, gcol, grow, grow, state_c, state_n, state_m5, hbuf)


def _mlstm_out_kernel(h_ref, o_ref, g_ref, out_ref, *, heads, dv):
    for hd in range(heads):
        sl = slice(hd * dv, (hd + 1) * dv)
        y = _rms(h_ref[:, sl]) * g_ref[hd:hd + 1, :]
        out_ref[:, sl] = (y * jax.nn.sigmoid(o_ref[:, sl].astype(F32))).astype(BF16)


def mlstm_out(hsum, z, g_head3, jl, heads, dv, tm=256):
    n, d = hsum.shape
    ocol = (z.shape[1] - d) // d
    return pl.pallas_call(
        functools.partial(_mlstm_out_kernel, heads=heads, dv=dv),
        grid=(n // tm,),
        in_specs=[
            pl.BlockSpec((tm, d), lambda i: (i, 0)),
            pl.BlockSpec((tm, d), lambda i: (i, ocol)),
            pl.BlockSpec((None, heads, dv), lambda i: (jl, 0, 0)),
        ],
        out_specs=pl.BlockSpec((tm, d), lambda i: (i, 0)),
        out_shape=jax.ShapeDtypeStruct((n, d), BF16),
        compiler_params=_cparams(("arbitrary",)),
    )(hsum, z, g_head3)


def _rope_kernel(q_ref, k_ref, c_ref, s_ref, qo_ref, ko_ref):
    cos = c_ref[...]
    sin = s_ref[...]
    lane = lax.broadcasted_iota(I32, cos.shape, 1)
    low = (lane & 32) == 0
    for ref, oref in ((q_ref, qo_ref), (k_ref, ko_ref)):
        for g in range(ref.shape[1] // V7X_LANES):
            sl = slice(g * V7X_LANES, (g + 1) * V7X_LANES)
            x = ref[:, sl]
            partner = jnp.where(low, pltpu.roll(x, V7X_LANES - 32, 1), pltpu.roll(x, 32, 1))
            oref[:, sl] = (x * cos + partner * sin).astype(BF16)


def rope_tables(ts, dh):
    half = dh // 2
    t = jnp.arange(ts)
    row = (t // GRID_W).astype(F32)
    col = (t % GRID_W).astype(F32)
    inv = ROPE_THETA ** (-jnp.arange(0, half, 2, dtype=F32) / half)
    ar = row[:, None] * inv
    ac = col[:, None] * inv
    cos = jnp.concatenate([jnp.cos(ar), jnp.cos(ar), jnp.cos(ac), jnp.cos(ac)], axis=-1)
    sin = jnp.concatenate([-jnp.sin(ar), jnp.sin(ar), -jnp.sin(ac), jnp.sin(ac)], axis=-1)
    return cos, sin


def rope_sample(qkv, cos, sin, geo, d, tm):
    off = geo.np_ // tm
    per = geo.ts // tm
    return pl.pallas_call(
        _rope_kernel,
        grid=(geo.ns // tm,),
        in_specs=[
            pl.BlockSpec((tm, d), lambda i: (off + i, 0)),
            pl.BlockSpec((tm, d), lambda i: (off + i, 1)),
            pl.BlockSpec((tm, V7X_LANES), lambda i: (i % per, 0)),
            pl.BlockSpec((tm, V7X_LANES), lambda i: (i % per, 0)),
        ],
        out_specs=[pl.BlockSpec((tm, d), lambda i: (i, 0)),
                   pl.BlockSpec((tm, d), lambda i: (i, 0))],
        out_shape=[jax.ShapeDtypeStruct((geo.ns, d), BF16),
                   jax.ShapeDtypeStruct((geo.ns, d), BF16)],
        compiler_params=_cparams(("arbitrary",)),
    )(qkv, qkv, cos, sin)


def _attn_kernel(*refs, dh, lam_init, cached):
    if cached:
        (lam_ref, q_ref, k_ref, v_ref, kc_ref, vc_ref, g_ref, _, o_ref,
         kk_sc, vv_sc, s_sc, e_sc) = refs
        new_kv = pl.program_id(2) == 0
    else:
        lam_ref, q_ref, k_ref, v_ref, g_ref, o_ref, kk_sc, vv_sc, s_sc, e_sc = refs
        new_kv = True
    s_new = k_ref.shape[0]

    @pl.when(new_kv)
    def _():
        kk_sc[0:s_new, :] = k_ref[...].astype(BF16)
        vv_sc[0:s_new, :] = v_ref[...].astype(BF16)
        if cached:
            kk_sc[s_new:, :] = kc_ref[...].astype(BF16)
            vv_sc[s_new:, :] = vc_ref[...].astype(BF16)

    lp = lam_ref[...]
    lam = (jnp.exp(jnp.sum(lp[0:1] * lp[1:2], axis=1, keepdims=True))
           - jnp.exp(jnp.sum(lp[2:3] * lp[3:4], axis=1, keepdims=True)) + lam_init)
    c2 = dh ** -0.5 * math.log2(math.e)
    tq = q_ref.shape[0]
    ck = V7X_LANES
    n_ck = kk_sc.shape[0] // ck
    rt = min(tq, ATTN_RT)
    n_rt = tq // rt

    def scores(c, r, j):
        q = q_ref[r * rt:(r + 1) * rt, c * dh:(c + 1) * dh].astype(BF16)
        s = lax.dot_general(q, kk_sc[j * ck:(j + 1) * ck, c * dh:(c + 1) * dh], NT_DIMS,
                            preferred_element_type=F32)
        s_sc[c, r * rt:(r + 1) * rt, j * ck:(j + 1) * ck] = s
        return s

    def probs(c, r, j, m):
        e = jnp.exp2((s_sc[c, r * rt:(r + 1) * rt, j * ck:(j + 1) * ck] - m) * c2)
        e_sc[c, r * rt:(r + 1) * rt, j * ck:(j + 1) * ck] = e.astype(BF16)
        return e

    def row_max(c, r):
        mv = jnp.full((rt, ck), -jnp.inf, F32)
        for j in range(n_ck):
            mv = jnp.maximum(mv, scores(c, r, j))
        return jnp.max(mv, axis=1, keepdims=True)

    def row_sum(c, r, m):
        lv = jnp.zeros((rt, ck), F32)
        for j in range(n_ck):
            lv = lv + probs(c, r, j, m)
        return jnp.sum(lv, axis=1, keepdims=True)

    def pv(c, ls):
        l = ls[0] if n_rt == 1 else jnp.concatenate(ls, axis=0)
        return jnp.dot(e_sc[c], vv_sc[...], preferred_element_type=F32) * (1.0 / l)

    order = [(c, r) for c in range(2) for r in range(n_rt)]
    sums = {0: [], 1: []}
    m_prev = row_max(*order[0])
    for idx, (c, r) in enumerate(order):
        m_next = row_max(*order[idx + 1]) if idx + 1 < len(order) else None
        sums[c].append(row_sum(c, r, m_prev))
        m_prev = m_next
    o = pv(0, sums[0]) - lam * pv(1, sums[1])
    o_ref[...] = (_rms(o) * g_ref[...] * (1.0 - lam_init)).astype(BF16)


def _attn_scratch(tq, s_total, w):
    return [pltpu.VMEM((s_total, w), BF16), pltpu.VMEM((s_total, w), BF16),
            pltpu.VMEM((2, tq, s_total), F32), pltpu.VMEM((2, tq, s_total), BF16)]


def attn_prompt(qkv, a_lam, g_sub4, jl, geo, heads, dh, lam_init, out_rows):
    t = geo.tp
    w = 2 * dh
    return pl.pallas_call(
        functools.partial(_attn_kernel, dh=dh, lam_init=lam_init, cached=False),
        grid=(geo.bp, heads),
        in_specs=[
            pl.BlockSpec((None, 4, dh), lambda b, h: (jl, 0, 0)),
            pl.BlockSpec((t, w), lambda b, h: (b, h)),
            pl.BlockSpec((t, w), lambda b, h: (b, heads + h)),
            pl.BlockSpec((t, w), lambda b, h: (b, 2 * heads + h)),
            pl.BlockSpec((None, None, 1, w), lambda b, h: (jl, h, 0, 0)),
        ],
        out_specs=pl.BlockSpec((t, w), lambda b, h: (b, h)),
        out_shape=jax.ShapeDtypeStruct((out_rows, heads * w), BF16),
        scratch_shapes=_attn_scratch(t, t, w),
        compiler_params=_cparams(("arbitrary", "arbitrary")),
    )(a_lam, qkv, qkv, qkv, g_sub4)


def attn_sample(q_rot, k_rot, qkv, cache_k4, cache_v4, a_lam, g_sub4, jl, geo, heads, dh,
                lam_init, prev, tq):
    w = 2 * dh
    nq = geo.ts // tq
    offq = geo.np_ // tq
    offs = geo.np_ // geo.ts
    past = cache_k4.shape[2]
    return pl.pallas_call(
        functools.partial(_attn_kernel, dh=dh, lam_init=lam_init, cached=True),
        grid=(geo.bs, heads, nq),
        in_specs=[
            pl.BlockSpec((None, 4, dh), lambda b, h, i: (jl, 0, 0)),
            pl.BlockSpec((tq, w), lambda b, h, i: (b * nq + i, h)),
            pl.BlockSpec((geo.ts, w), lambda b, h, i: (b, h)),
            pl.BlockSpec((geo.ts, w), lambda b, h, i: (offs + b, 2 * heads + h)),
            pl.BlockSpec((None, None, past, w), lambda b, h, i: (b, jl, 0, h)),
            pl.BlockSpec((None, None, past, w), lambda b, h, i: (b, jl, 0, h)),
            pl.BlockSpec((None, None, 1, w), lambda b, h, i: (jl, h, 0, 0)),
            pl.BlockSpec(memory_space=pl.ANY),
        ],
        out_specs=pl.BlockSpec((tq, w), lambda b, h, i: (offq + b * nq + i, h)),
        out_shape=jax.ShapeDtypeStruct(prev.shape, BF16),
        input_output_aliases={7: 0},
        scratch_shapes=_attn_scratch(tq, geo.ts + past, w),
        compiler_params=_cparams(("arbitrary", "arbitrary", "arbitrary")),
    )(a_lam, q_rot, k_rot, qkv, cache_k4, cache_v4, g_sub4, prev)


def _router_kernel(x_ref, g_ref, sh_ref, sc_ref, rwh_ref, rwl_ref, rb_ref,
                   h_ref, ei_ref, rk_ref, gwc_ref, cnt_ref, carry_sc, *, n_exp):
    i = pl.program_id(0)

    @pl.when(i == 0)
    def _():
        carry_sc[...] = jnp.zeros_like(carry_sc)

    y = _rms(x_ref[...]) * g_ref[...]
    hm = y * (1.0 + sc_ref[...]) + sh_ref[...]
    h_ref[...] = hm
    tm = hm.shape[0]
    hi = hm.astype(BF16)
    lo = (hm - hi.astype(F32)).astype(BF16)
    rwh = rwh_ref[...]
    logits = (lax.dot_general(rwh, hi, NT_DIMS, preferred_element_type=F32)
              + lax.dot_general(rwh, lo, NT_DIMS, preferred_element_type=F32)
              + lax.dot_general(rwl_ref[...], hi, NT_DIMS, preferred_element_type=F32))
    ex = jnp.exp(logits - jnp.max(logits, axis=0, keepdims=True))
    probs = ex / jnp.sum(ex, axis=0, keepdims=True)
    sel = probs + rb_ref[...]
    epg = n_exp // N_GROUPS
    sub = lax.broadcasted_iota(I32, (epg, tm), 0).astype(F32)
    ninf = -jnp.inf

    def top2(v):
        m1 = jnp.max(v, axis=0, keepdims=True)
        a1 = jnp.min(jnp.where(v == m1, sub, float(epg)), axis=0, keepdims=True)
        v2 = jnp.where(sub == a1, ninf, v)
        m2 = jnp.max(v2, axis=0, keepdims=True)
        a2 = jnp.min(jnp.where(v2 == m2, sub, float(epg)), axis=0, keepdims=True)
        return m1, a1, m2, a2

    best = None
    gi = None
    for g in range(N_GROUPS):
        m1, _, m2, _ = top2(sel[g * epg:(g + 1) * epg, :])
        score = m1 + m2
        if g == 0:
            best, gi = score, jnp.zeros((1, tm), F32)
        else:
            upd = score > best
            best = jnp.where(upd, score, best)
            gi = jnp.where(upd, float(g), gi)
    ing = jnp.zeros((epg, tm), F32)
    pin = jnp.zeros((epg, tm), F32)
    for g in range(N_GROUPS):
        pick = gi == float(g)
        ing = jnp.where(pick, sel[g * epg:(g + 1) * epg, :], ing)
        pin = jnp.where(pick, probs[g * epg:(g + 1) * epg, :], pin)
    _, l1, _, l2 = top2(ing)
    w1 = jnp.sum(jnp.where(sub == l1, pin, 0.0), axis=0, keepdims=True)
    w2 = jnp.sum(jnp.where(sub == l2, pin, 0.0), axis=0, keepdims=True)
    ws = w1 + w2
    e1 = (gi * epg + l1).astype(I32)
    e2 = (gi * epg + l2).astype(I32)

    eio = lax.broadcasted_iota(I32, (n_exp, tm), 0)
    is1 = eio == e1
    is2 = eio == e2
    mem = jnp.where(is1 | is2, 1.0, 0.0)
    tri = jnp.where(lax.broadcasted_iota(I32, (tm, tm), 0) < lax.broadcasted_iota(I32, (tm, tm), 1),
                    1.0, 0.0).astype(BF16)
    pre = jnp.dot(mem.astype(BF16), tri, preferred_element_type=F32) + carry_sc[...]
    r1 = jnp.sum(jnp.where(is1, pre, 0.0), axis=0, keepdims=True)
    r2 = jnp.sum(jnp.where(is2, pre, 0.0), axis=0, keepdims=True)
    carry = carry_sc[...] + jnp.sum(mem, axis=1, keepdims=True)
    carry_sc[...] = carry
    cnt_ref[...] = jnp.broadcast_to(carry, cnt_ref.shape)

    row8 = lax.broadcasted_iota(I32, (V7X_SUBLANES, tm), 0)
    ei_ref[...] = jnp.where(row8 == 0, e1, jnp.where(row8 == 1, e2, 0))
    rk_ref[...] = jnp.where(row8 == 0, r1.astype(I32), jnp.where(row8 == 1, r2.astype(I32), 0))
    row128 = lax.broadcasted_iota(I32, (V7X_LANES, tm), 0)
    gw = jnp.where(row128 == 0, w1 / ws, jnp.where(row128 == 1, w2 / ws, 0.0))
    gwc_ref[...] = gw.T


def moe_router(x, g_norm4, mod, rwt_hi, rwt_lo, rb_col, geo, layer, tm=256):
    n, d = x.shape
    n_exp = rwt_hi.shape[0]
    return pl.pallas_call(
        functools.partial(_router_kernel, n_exp=n_exp),
        grid=(n // tm,),
        in_specs=[
            pl.BlockSpec((tm, d), lambda i: (i, 0)),
            pl.BlockSpec((None, None, 1, d), lambda i: (layer, 1, 0, 0)),
            pl.BlockSpec((None, None, None, 1, d), lambda i: (layer, geo.mod_row(i, tm), 3, 0, 0)),
            pl.BlockSpec((None, None, None, 1, d), lambda i: (layer, geo.mod_row(i, tm), 4, 0, 0)),
            pl.BlockSpec((n_exp, d), lambda i: (0, 0)),
            pl.BlockSpec((n_exp, d), lambda i: (0, 0)),
            pl.BlockSpec((n_exp, 1), lambda i: (0, 0)),
        ],
        out_specs=[
            pl.BlockSpec((tm, d), lambda i: (i, 0)),
            pl.BlockSpec((V7X_SUBLANES, tm), lambda i: (0, i)),
            pl.BlockSpec((V7X_SUBLANES, tm), lambda i: (0, i)),
            pl.BlockSpec((tm, V7X_LANES), lambda i: (i, 0)),
            pl.BlockSpec((n_exp, V7X_LANES), lambda i: (0, 0)),
        ],
        out_shape=[
            jax.ShapeDtypeStruct((n, d), F32),
            jax.ShapeDtypeStruct((V7X_SUBLANES, n), I32),
            jax.ShapeDtypeStruct((V7X_SUBLANES, n), I32),
            jax.ShapeDtypeStruct((n, V7X_LANES), F32),
            jax.ShapeDtypeStruct((n_exp, V7X_LANES), F32),
        ],
        scratch_shapes=[pltpu.VMEM((n_exp, 1), F32)],
        compiler_params=_cparams(("arbitrary",)),
    )(x, g_norm4, mod, mod, rwt_hi, rwt_lo, rb_col)


def _gather_kernel(tok_ref, nused_ref, h_hbm, o_ref, buf, sem, *, bm):
    blk = pl.program_id(0)
    n_used = nused_ref[0]

    def row_copy(b, r):
        slot = b % 2
        return pltpu.make_async_copy(h_hbm.at[pl.ds(tok_ref[b * bm + r], 1)],
                                     buf.at[slot, pl.ds(r, 1)], sem.at[slot])

    def start_block(b):
        def start(r, carry):
            row_copy(b, r).start()
            return carry
        lax.fori_loop(0, bm, start, 0, unroll=8)

    @pl.when(blk == 0)
    def _():
        start_block(blk)

    @pl.when(blk + 1 < n_used)
    def _():
        start_block(blk + 1)

    @pl.when(blk < n_used)
    def _():
        def wait(r, carry):
            row_copy(blk, r).wait()
            return carry
        lax.fori_loop(0, bm, wait, 0, unroll=8)
        o_ref[...] = buf[blk % 2].astype(BF16)


def moe_gather(h2, slot_tok, n_used, n_blocks, bm):
    n, d = h2.shape
    return pl.pallas_call(
        functools.partial(_gather_kernel, bm=bm),
        grid_spec=pltpu.PrefetchScalarGridSpec(
            num_scalar_prefetch=2,
            grid=(n_blocks,),
            in_specs=[pl.BlockSpec(memory_space=pl.ANY)],
            out_specs=pl.BlockSpec((bm, d), lambda b, tok, nu: (b, 0)),
            scratch_shapes=[pltpu.VMEM((2, bm, d), F32), pltpu.SemaphoreType.DMA((2,))],
        ),
        out_shape=jax.ShapeDtypeStruct((n_blocks * bm, d), BF16),
        compiler_params=_cparams(("arbitrary",)),
    )(slot_tok, n_used, h2)


def _moe_up_kernel(blk_ref, f_ref, e_ref, first_ref, valid_ref,
                   x_ref, wg_ref, wu_ref, a_ref, wg_sc, wu_sc):
    w = pl.program_id(0)

    @pl.when(valid_ref[w] == 1)
    def _():
        @pl.when(first_ref[w] == 1)
        def _():
            wg_sc[...] = wg_ref[...].astype(BF16)
            wu_sc[...] = wu_ref[...].astype(BF16)

        x = x_ref[...]
        g = jnp.dot(x, wg_sc[...], preferred_element_type=F32)
        u = jnp.dot(x, wu_sc[...], preferred_element_type=F32)
        a_ref[...] = (_silu(g) * u).astype(BF16)


def moe_up(xs, tabs, w_gate, w_up, layer, bm, tf):
    n_slots, d = xs.shape
    ff = w_gate.shape[3]
    nw = tabs[0].shape[0]
    wspec = pl.BlockSpec((None, None, d, tf), lambda w, blk, f, e, fi, va: (layer, e[w], 0, f[w]))
    return pl.pallas_call(
        _moe_up_kernel,
        grid_spec=pltpu.PrefetchScalarGridSpec(
            num_scalar_prefetch=5,
            grid=(nw,),
            in_specs=[pl.BlockSpec((bm, d), lambda w, blk, f, e, fi, va: (blk[w], 0)), wspec, wspec],
            out_specs=pl.BlockSpec((bm, tf), lambda w, blk, f, e, fi, va: (blk[w], f[w])),
            scratch_shapes=[pltpu.VMEM((d, tf), BF16), pltpu.VMEM((d, tf), BF16)],
        ),
        out_shape=jax.ShapeDtypeStruct((n_slots, ff), BF16),
        compiler_params=_cparams(("arbitrary",)),
    )(*tabs, xs, w_gate, w_up)


def _moe_down_kernel(blk_ref, f_ref, e_ref, first_ref, valid_ref, a_ref, wd_ref, y_ref, wd_sc):
    w = pl.program_id(0)

    @pl.when(valid_ref[w] == 1)
    def _():
        @pl.when(first_ref[w] == 1)
        def _():
            wd_sc[...] = wd_ref[...].astype(BF16)

        y_ref[...] = jnp.dot(a_ref[...], wd_sc[...], preferred_element_type=F32)


def moe_down(act, tabs, w_down, layer, bm, tn):
    n_slots, ff = act.shape
    d = w_down.shape[3]
    nw = tabs[0].shape[0]
    return pl.pallas_call(
        _moe_down_kernel,
        grid_spec=pltpu.PrefetchScalarGridSpec(
            num_scalar_prefetch=5,
            grid=(nw,),
            in_specs=[
                pl.BlockSpec((bm, ff), lambda w, blk, f, e, fi, va: (blk[w], 0)),
                pl.BlockSpec((None, None, ff, tn), lambda w, blk, f, e, fi, va: (layer, e[w], 0, f[w])),
            ],
            out_specs=pl.BlockSpec((bm, tn), lambda w, blk, f, e, fi, va: (blk[w], f[w])),
            scratch_shapes=[pltpu.VMEM((ff, tn), BF16)],
        ),
        out_shape=jax.ShapeDtypeStruct((n_slots, d), F32),
        compiler_params=_cparams(("arbitrary",)),
    )(*tabs, act, w_down)


def _work_tables(nb_e, n_inner, n_blocks):
    n_exp = nb_e.shape[0]
    s_e = jnp.cumsum(nb_e) - nb_e
    tot = jnp.sum(nb_e)
    wstart = n_inner * s_e
    wend = wstart + n_inner * nb_e
    nw = n_inner * n_blocks
    n_valid = n_inner * tot
    w = jnp.minimum(jnp.arange(nw, dtype=I32), n_valid - 1)
    e = jnp.minimum(jnp.sum((wend[None, :] <= w[:, None]).astype(I32), axis=1), n_exp - 1)
    r = w - wstart[e]
    nb = jnp.maximum(nb_e[e], 1)
    f = r // nb
    p = r % nb
    blk = s_e[e] + p
    valid = (jnp.arange(nw, dtype=I32) < n_valid).astype(I32)
    first = valid * (p == 0).astype(I32)
    return (blk.astype(I32), f.astype(I32), e, first, valid)


def _combine_kernel(dest_ref, x_ref, g_ref, gwc_ref, y_hbm, o_ref, buf, sem, *, tb, n_tok):
    i = pl.program_id(0)

    def row_copy(blk, k, r):
        slot = blk % 2
        return pltpu.make_async_copy(y_hbm.at[pl.ds(dest_ref[k * n_tok + blk * tb + r], 1)],
                                     buf.at[slot, k, pl.ds(r, 1)], sem.at[slot])

    def start_block(blk):
        def start(r, carry):
            row_copy(blk, 0, r).start()
            row_copy(blk, 1, r).start()
            return carry
        lax.fori_loop(0, tb, start, 0, unroll=8)

    @pl.when(i == 0)
    def _():
        start_block(i)

    @pl.when(i + 1 < pl.num_programs(0))
    def _():
        start_block(i + 1)

    def wait(r, carry):
        row_copy(i, 0, r).wait()
        row_copy(i, 1, r).wait()
        return carry

    lax.fori_loop(0, tb, wait, 0, unroll=8)
    slot = i % 2
    gwc = gwc_ref[...]
    y = gwc[:, 0:1] * buf[slot, 0] + gwc[:, 1:2] * buf[slot, 1]
    o_ref[...] = x_ref[...] + g_ref[...] * y


def moe_combine(x, yb, dest, gwc, mod, geo, layer, tb=256):
    n, d = x.shape
    return pl.pallas_call(
        functools.partial(_combine_kernel, tb=tb, n_tok=n),
        grid_spec=pltpu.PrefetchScalarGridSpec(
            num_scalar_prefetch=1,
            grid=(n // tb,),
            in_specs=[
                pl.BlockSpec((tb, d), lambda i, dr: (i, 0)),
                pl.BlockSpec((None, None, None, 1, d), lambda i, dr: (layer, geo.mod_row(i, tb), 5, 0, 0)),
                pl.BlockSpec((tb, V7X_LANES), lambda i, dr: (i, 0)),
                pl.BlockSpec(memory_space=pl.ANY),
            ],
            out_specs=pl.BlockSpec((tb, d), lambda i, dr: (i, 0)),
            scratch_shapes=[pltpu.VMEM((2, 2, tb, d), F32), pltpu.SemaphoreType.DMA((2,))],
        ),
        out_shape=jax.ShapeDtypeStruct((n, d), F32),
        compiler_params=_cparams(("arbitrary",)),
    )(dest, x, mod, gwc, yb)


def moe_layer(x, g_norm4, mod, rwt_hi, rwt_lo, rb_col, w_gate, w_up, w_down, geo, layer):
    n, d = x.shape
    n_exp = rwt_hi.shape[0]
    bm = MOE_BM
    h2, ei, rk, gwc, cnt = moe_router(x, g_norm4, mod, rwt_hi, rwt_lo, rb_col, geo, layer)
    counts = cnt[:, 0].astype(I32)
    nb_e = (counts + bm - 1) // bm
    pad_start = (jnp.cumsum(nb_e) - nb_e) * bm
    e_flat = ei[:2].reshape(-1)
    dest = pad_start[e_flat] + rk[:2].reshape(-1)
    n_blocks = (2 * n) // bm + n_exp
    tok = jnp.tile(jnp.arange(n, dtype=I32), 2)
    slot_tok = jnp.zeros((n_blocks * bm,), I32).at[dest].set(tok)
    n_used = jnp.sum(nb_e).astype(I32).reshape(1)
    xs = moe_gather(h2, slot_tok, n_used, n_blocks, bm)
    ff = w_gate.shape[3]
    tf = min(MOE_TF, ff)
    act = moe_up(xs, _work_tables(nb_e, ff // tf, n_blocks), w_gate, w_up, layer, bm, tf)
    yb = moe_down(act, _work_tables(nb_e, 1, n_blocks), w_down, layer, bm, d)
    return moe_combine(x, yb, dest, gwc, mod, geo, layer)


def _final_kernel(x_ref, g_ref, o_ref):
    o_ref[...] = _rms(x_ref[...]) * g_ref[...]


def final_norm(x, g_row, row0, rows, tm=256):
    d = x.shape[1]
    off = row0 // tm
    return pl.pallas_call(
        _final_kernel,
        grid=(rows // tm,),
        in_specs=[pl.BlockSpec((tm, d), lambda i: (off + i, 0)),
                  pl.BlockSpec((1, d), lambda i: (0, 0))],
        out_specs=pl.BlockSpec((tm, d), lambda i: (i, 0)),
        out_shape=jax.ShapeDtypeStruct((rows, d), F32),
        compiler_params=_cparams(("arbitrary",)),
    )(x, g_row)


def _gate_weights(w_in, b_gate, n_main, heads):
    j = jnp.arange(V7X_LANES) // V7X_SUBLANES
    d_, h_ = j // heads, j % heads
    ci = n_main + d_ * 2 * heads + h_
    cf = ci + heads
    wi = w_in[:, ci].astype(BF16)
    wf = w_in[:, cf].astype(BF16)
    bi = b_gate[ci - n_main].reshape(1, -1)
    bf = b_gate[cf - n_main].reshape(1, -1)
    return wi, wf, bi, bf


def kernel(x_prompt, x_sample, state_C, state_n, state_m, cache_k, cache_v, c, c_ctx,
           w_mod, b_mod, g_norm, m_w_in, m_b_gate, m_g_head, m_w_out,
           a_w_qkv, a_lam, a_g_sub, a_w_o, router_w, router_b, w_gate, w_up, w_down, g_final):
    bp, tp, d = x_prompt.shape
    bs, ts, _ = x_sample.shape
    depth = w_mod.shape[0]
    geo = Geo(bp, tp, bs, ts)
    tm = geo.tm
    m_heads, m_dv = m_g_head.shape[1], m_g_head.shape[2]
    m_dk = m_dv // 2
    a_heads = a_g_sub.shape[1]
    a_dh = a_g_sub.shape[2] // 2
    n_exp = router_w.shape[1]
    assert tp == MLSTM_CHUNK and ts % MLSTM_CHUNK == 0 and geo.np_ % ts == 0
    assert 2 * m_heads * V7X_SUBLANES == V7X_LANES and n_exp // N_GROUPS == V7X_SUBLANES

    x = jnp.concatenate([x_prompt.reshape(geo.np_, d), x_sample.reshape(geo.ns, d)], axis=0)

    n_rows = 2 * V7X_SUBLANES
    cv = jnp.zeros((n_rows, d), F32).at[0].set(c_ctx).at[1:1 + bs].set(c)
    mod = adaln_all(cv, w_mod, b_mod).reshape(depth, n_rows, 6, 1, d)
    g_norm4 = g_norm.reshape(depth, 2, 1, d)

    rwt = router_w.T
    rwt_hi = rwt.astype(BF16)
    rwt_lo = (rwt - rwt_hi.astype(F32)).astype(BF16)
    rb_col = router_b.reshape(n_exp, 1)

    new_c, new_n, new_m, new_k, new_v = [], [], [], [], []
    for i in range(depth):
        jl = i // N_MIXERS
        h = modnorm(x, g_norm4, mod, geo, i, 0, 0, 1)
        if i % N_MIXERS == 0:
            n_main = 2 * m_heads * m_dk + 2 * m_heads * m_dv
            z = matmul(h, m_w_in, jl, n_main, tm, out_dtype=BF16)
            wi, wf, bi, bf = _gate_weights(m_w_in[jl], m_b_gate[jl], n_main, m_heads)
            L = MLSTM_CHUNK
            gcol, grow = mlstm_gates(h, wi, wf, bi, bf, L, tm)
            hsum, c_new, n_new, m_new = mlstm_prompt(z, gcol, grow, geo, m_heads, m_dk, m_dv)
            new_c.append(c_new)
            new_n.append(n_new)
            new_m.append(m_new.reshape(bp, 2, m_heads))
            state_m5 = state_m.reshape(state_m.shape + (1, 1))
            hsum = mlstm_sample(z, gcol, grow, state_C, state_n, state_m5, jl, geo, m_heads,
                                m_dk, m_dv, L, hsum)
            mix_in = mlstm_out(hsum, z, m_g_head, jl, m_heads, m_dv)
            x = matmul_residual(mix_in, m_w_out, jl, x, mod, geo, i, 2, tm)
        else:
            lam_init = 0.8 - 0.6 * math.exp(-0.3 * i)
            qkv = matmul(h, a_w_qkv, jl, 3 * d, tm)
            new_k.append(qkv[:geo.np_, d:2 * d].reshape(bp, tp, a_heads, 2, a_dh))
            new_v.append(qkv[:geo.np_, 2 * d:].reshape(bp, tp, a_heads, 2 * a_dh))
            g_sub4 = a_g_sub.reshape(a_g_sub.shape[0], a_heads, 1, 2 * a_dh)
            cos, sin = rope_tables(ts, a_dh)
            q_rot, k_rot = rope_sample(qkv, cos, sin, geo, d, min(512, ts))
            att = attn_prompt(qkv, a_lam, g_sub4, jl, geo, a_heads, a_dh, lam_init, geo.n)
            ck4 = cache_k.reshape(bs, cache_k.shape[1], cache_k.shape[2], d)
            cv4 = cache_v.reshape(bs, cache_v.shape[1], cache_v.shape[2], d)
            att = attn_sample(q_rot, k_rot, qkv, ck4, cv4, a_lam, g_sub4, jl, geo, a_heads, a_dh,
                              lam_init, att, min(ATTN_TQ, ts))
            x = matmul_residual(att, a_w_o, jl, x, mod, geo, i, 2, tm)
        x = moe_layer(x, g_norm4, mod, rwt_hi, rwt_lo, rb_col, w_gate, w_up, w_down, geo, i)

    g_row = g_final.reshape(1, d)
    y_prompt = final_norm(x, g_row, 0, geo.np_).reshape(bp, tp, d)
    y_sample = final_norm(x, g_row, geo.np_, geo.ns).reshape(bs, ts, d)
    return (y_prompt, y_sample,
            jnp.stack(new_c, axis=1), jnp.stack(new_n, axis=1), jnp.stack(new_m, axis=1),
            jnp.stack(new_k, axis=1), jnp.stack(new_v, axis=1))
```
